```python
import math
import jax, jax.numpy as jnp
from jax import lax
import numpy as np

D_MODEL = 1024
BATCH = 4
SEQ = 8192
DEPTH = 2

MIX_WIDTH = D_MODEL
DIFF_HEADS = 4
DIFF_D = D_MODEL // 16
DIFF_DV = 2 * DIFF_D
GLA_HEADS = 4
GLA_DV = D_MODEL // 8
GLA_DK = GLA_DV // 2
GLA_GATE_RANK = 16
GLA_GATE_NORMALIZER = 16.0
GLA_CHUNK = 64
PEER_HEADS = 8
PEER_N_KEYS = 128
PEER_N_EXPERTS = PEER_N_KEYS * PEER_N_KEYS
PEER_D_KEY = 256
PEER_TOPK = 16
PEER_TOKEN_BLOCK = 128
ATTN_Q_BLOCK = 128
RMS_EPS = 1e-6

IN_SIZES = (
    DIFF_HEADS * 2 * DIFF_D,
    DIFF_HEADS * 2 * DIFF_D,
    DIFF_HEADS * DIFF_DV,
    GLA_HEADS * GLA_DK,
    GLA_HEADS * GLA_DK,
    GLA_HEADS * GLA_DV,
    GLA_HEADS * GLA_DV,
    GLA_GATE_RANK,
)
IN_COLS = sum(IN_SIZES)

kernel_name = "hymba_diffattn_gla_peer"


def _split_points():
    pts, acc = [], 0
    for s in IN_SIZES[:-1]:
        acc += s
        pts.append(acc)
    return pts


def rms_norm(x, w):
    xf = x.astype(jnp.float32)
    y = xf * lax.rsqrt(jnp.mean(xf * xf, axis=-1, keepdims=True) + RMS_EPS)
    return (y * w.astype(jnp.float32)).astype(x.dtype)


def diff_attention(q, k, v, q_norm_w, k_norm_w, lam, lam_init, out_norm_w):
    B, T = q.shape[0], q.shape[1]
    q = rms_norm(q, q_norm_w) * (DIFF_D ** -0.5)
    k = rms_norm(k, k_norm_w)
    nb = T // ATTN_Q_BLOCK
    qb = q.reshape(B, nb, ATTN_Q_BLOCK, DIFF_HEADS, 2, DIFF_D).transpose(1, 0, 2, 3, 4, 5)
    kpos = jnp.arange(T)

    def block(args):
        qi, i = args
        s = jnp.einsum('bqhmd,bkhmd->bhmqk', qi, k, preferred_element_type=jnp.float32)
        qpos = i * ATTN_Q_BLOCK + jnp.arange(ATTN_Q_BLOCK)
        mask = kpos[None, :] <= qpos[:, None]
        p = jax.nn.softmax(jnp.where(mask, s, -jnp.inf), axis=-1)
        a = p[:, :, 0] - lam * p[:, :, 1]
        return jnp.einsum('bhqk,bkhe->bqhe', a.astype(v.dtype), v)

    o = lax.map(block, (qb, jnp.arange(nb)))
    o = o.transpose(1, 0, 2, 3, 4).reshape(B, T, DIFF_HEADS, DIFF_DV)
    o = rms_norm(o, out_norm_w) * (1.0 - lam_init)
    return o.reshape(B, T, DIFF_HEADS * DIFF_DV)


def gla_chunked(q, k, v, g_log, out_gate, out_norm_w):
    B, T = q.shape[0], q.shape[1]
    C = GLA_CHUNK
    n = T // C
    out_dtype = v.dtype

    def chunks(a):
        return a.astype(jnp.float32).reshape(B, n, C, GLA_HEADS, a.shape[-1]).transpose(0, 3, 1, 2, 4)

    qc = chunks(q) * (GLA_DK ** -0.5)
    kc, vc, gc = chunks(k), chunks(v), chunks(g_log)
    b = jnp.cumsum(gc, axis=3)
    b_last = b[:, :, :, -1:, :]
    q_dec = qc * jnp.exp(b)
    k_inv = kc * jnp.exp(-b)
    k_dec = kc * jnp.exp(b_last - b)
    causal = jnp.tril(jnp.ones((C, C), dtype=bool))
    A = jnp.where(causal, jnp.einsum('bhnid,bhnjd->bhnij', q_dec, k_inv), 0.0)
    o_intra = jnp.einsum('bhnij,bhnje->bhnie', A, vc)
    U = jnp.einsum('bhncd,bhnce->nbhde', k_dec, vc)
    decay = jnp.exp(b_last[:, :, :, 0, :]).transpose(2, 0, 1, 3)

    def step(S, inp):
        dec, u = inp
        return dec[..., None] * S + u, S

    S0 = jnp.zeros((B, GLA_HEADS, GLA_DK, GLA_DV), jnp.float32)
    _, S_prev = lax.scan(step, S0, (decay, U))
    o_inter = jnp.einsum('bhncd,nbhde->bhnce', q_dec, S_prev)
    o = (o_intra + o_inter).transpose(0, 2, 3, 1, 4).reshape(B, T, GLA_HEADS, GLA_DV)
    o = rms_norm(o, out_norm_w) * jax.nn.silu(out_gate.astype(jnp.float32))
    return o.reshape(B, T, GLA_HEADS * GLA_DV).astype(out_dtype)


def peer(h, w_q, sub_keys, u_tab, v_tab):
    B, T, D = h.shape
    nt = (B * T) // PEER_TOKEN_BLOCK
    hb = h.reshape(nt, PEER_TOKEN_BLOCK, D)
    tb, kk = PEER_TOKEN_BLOCK, PEER_TOPK

    def block(xb):
        q = (xb @ w_q).reshape(tb, PEER_HEADS, 2, PEER_D_KEY // 2)
        s = jnp.einsum('thcd,hckd->thck', q, sub_keys, preferred_element_type=jnp.float32)
        top_s, top_i = lax.top_k(s, kk)
        cand = top_s[:, :, 0, :, None] + top_s[:, :, 1, None, :]
        cand_idx = top_i[:, :, 0, :, None] * PEER_N_KEYS + top_i[:, :, 1, None, :]
        best_s, best_pos = lax.top_k(cand.reshape(tb, PEER_HEADS, kk * kk), kk)
        expert = jnp.take_along_axis(cand_idx.reshape(tb, PEER_HEADS, kk * kk), best_pos, axis=-1)
        gate = jax.nn.softmax(best_s, axis=-1)
        u = u_tab[expert]
        act = jax.nn.gelu(jnp.einsum('td,thkd->thk', xb, u, preferred_element_type=jnp.float32),
                          approximate=False)
        w = (gate * act).astype(xb.dtype)
        return jnp.einsum('thk,thkd->td', w, v_tab[expert])

    return lax.map(block, hb).reshape(B, T, D)


def setup_inputs(seed: int = 0) -> dict:
    key = jax.random.key(seed)
    ks = jax.random.split(key, 20)

    def nrm(k, shape, scale):
        return jax.random.normal(k, shape, jnp.float32) * scale

    return {
        "x": nrm(ks[0], (BATCH, SEQ, D_MODEL), 1.0),
        "attn_norm_w": 1.0 + nrm(ks[1], (DEPTH, D_MODEL), 0.02),
        "w_in": nrm(ks[2], (DEPTH, D_MODEL, IN_COLS), D_MODEL ** -0.5),
        "diff_q_norm_w": 1.0 + nrm(ks[3], (DEPTH, DIFF_D), 0.02),
        "diff_k_norm_w": 1.0 + nrm(ks[4], (DEPTH, DIFF_D), 0.02),
        "lambda_q1": nrm(ks[5], (DEPTH, DIFF_D), 0.1),
        "lambda_k1": nrm(ks[6], (DEPTH, DIFF_D), 0.1),
        "lambda_q2": nrm(ks[7], (DEPTH, DIFF_D), 0.1),
        "lambda_k2": nrm(ks[8], (DEPTH, DIFF_D), 0.1),
        "diff_out_norm_w": 1.0 + nrm(ks[9], (DEPTH, DIFF_DV), 0.02),
        "gla_gate_w2": nrm(ks[10], (DEPTH, GLA_GATE_RANK, GLA_HEADS * GLA_DK), GLA_GATE_RANK ** -0.5),
        "gla_gate_b": nrm(ks[11], (DEPTH, GLA_HEADS * GLA_DK), 0.01),
        "gla_out_norm_w": 1.0 + nrm(ks[12], (DEPTH, GLA_DV), 0.02),
        "w_out": nrm(ks[13], (DEPTH, MIX_WIDTH, D_MODEL), MIX_WIDTH ** -0.5),
        "ffn_norm_w": 1.0 + nrm(ks[14], (DEPTH, D_MODEL), 0.02),
        "peer_w_q": nrm(ks[15], (DEPTH, D_MODEL, PEER_HEADS * PEER_D_KEY), D_MODEL ** -0.5),
        "peer_sub_keys": nrm(ks[16], (DEPTH, PEER_HEADS, 2, PEER_N_KEYS, PEER_D_KEY // 2),
                             (PEER_D_KEY // 2) ** -0.5),
        "peer_u": nrm(ks[17], (DEPTH, PEER_N_EXPERTS, D_MODEL), D_MODEL ** -0.5),
        "peer_v": nrm(ks[18], (DEPTH, PEER_N_EXPERTS, D_MODEL), PEER_HEADS ** -0.5),
    }


def reference(x, attn_norm_w, w_in, diff_q_norm_w, diff_k_norm_w, lambda_q1, lambda_k1,
              lambda_q2, lambda_k2, diff_out_norm_w, gla_gate_w2, gla_gate_b, gla_out_norm_w,
              w_out, ffn_norm_w, peer_w_q, peer_sub_keys, peer_u, peer_v):
    B, T, _ = x.shape
    split_pts = _split_points()
    for l in range(DEPTH):
        h = rms_norm(x, attn_norm_w[l])
        proj = h @ w_in[l]
        d_q, d_k, d_v, g_q, g_k, g_v, g_out, g_lr = jnp.split(proj, split_pts, axis=-1)

        lam_init = 0.8 - 0.6 * math.exp(-0.3 * l)
        lam = (jnp.exp(jnp.sum(lambda_q1[l].astype(jnp.float32) * lambda_k1[l].astype(jnp.float32)))
               - jnp.exp(jnp.sum(lambda_q2[l].astype(jnp.float32) * lambda_k2[l].astype(jnp.float32)))
               + lam_init)
        diff_o = diff_attention(
            d_q.reshape(B, T, DIFF_HEADS, 2, DIFF_D),
            d_k.reshape(B, T, DIFF_HEADS, 2, DIFF_D),
            d_v.reshape(B, T, DIFF_HEADS, DIFF_DV),
            diff_q_norm_w[l], diff_k_norm_w[l], lam, lam_init, diff_out_norm_w[l])

        g_log = jax.nn.log_sigmoid((g_lr @ gla_gate_w2[l] + gla_gate_b[l]).astype(jnp.float32)) \
            / GLA_GATE_NORMALIZER
        gla_o = gla_chunked(
            g_q.reshape(B, T, GLA_HEADS, GLA_DK),
            g_k.reshape(B, T, GLA_HEADS, GLA_DK),
            g_v.reshape(B, T, GLA_HEADS, GLA_DV),
            g_log.reshape(B, T, GLA_HEADS, GLA_DK),
            g_out.reshape(B, T, GLA_HEADS, GLA_DV),
            gla_out_norm_w[l])

        mix = jnp.concatenate([diff_o.astype(x.dtype), gla_o.astype(x.dtype)], axis=-1)
        x = x + mix @ w_out[l]

        h = rms_norm(x, ffn_norm_w[l])
        x = x + peer(h, peer_w_q[l], peer_sub_keys[l], peer_u[l], peer_v[l])
    return x
```

```python
import functools
import math

import jax
import jax.numpy as jnp
from jax import lax
from jax.experimental import pallas as pl
from jax.experimental.pallas import tpu as pltpu

F32 = jnp.float32
BF16 = jnp.bfloat16
HIGHEST = lax.Precision.HIGHEST

D_MODEL = 1024
DIFF_HEADS = 4
DIFF_D = 64
DIFF_DV = 128
GLA_HEADS = 4
GLA_DK = 64
GLA_DV = 128
GLA_GATE_RANK = 16
GLA_GATE_NORMALIZER = 16.0
GLA_CHUNK = 64
PEER_HEADS = 8
PEER_N_KEYS = 128
PEER_TOPK = 16
PEER_SLOTS = PEER_HEADS * PEER_TOPK
RMS_EPS = 1e-6

LANES = 128
PROJ_W = 3072
PROJ_PAD = PROJ_W + LANES

DQ_BLK, DK_BLK, DV_BLK = 0, 4, 8
GQ_BLK, GK_BLK = 6, 7
GV_BLK, GO_BLK = 4, 5

VMEM_LIMIT = 56 * 1024 * 1024

NT_DIMS = (((1,), (1,)), ((), ()))
TN_DIMS = (((0,), (0,)), ((), ()))


def _cparams(sem):
    return pltpu.CompilerParams(dimension_semantics=sem, vmem_limit_bytes=VMEM_LIMIT)


def _rms(x, w):
    ms = jnp.mean(x * x, axis=-1, keepdims=True)
    return x * lax.rsqrt(ms + RMS_EPS) * w


def _inproj_body(x_ref, nw_ref, w_ref, w2_ref, b_ref, proj_ref, glog_ref):
    h = _rms(x_ref[...], nw_ref[...])
    p = jnp.dot(h.astype(BF16), w_ref[...], preferred_element_type=F32)
    proj_ref[...] = p[:, :PROJ_W]
    g = jnp.dot(p[:, PROJ_W:], w2_ref[...], preferred_element_type=F32,
                precision=HIGHEST) + b_ref[...]
    log_sig = -(jnp.maximum(-g, 0.0) + jnp.log1p(jnp.exp(-jnp.abs(g))))
    glog_ref[...] = log_sig * (1.0 / GLA_GATE_NORMALIZER)


def _inproj(x2d, norm_w, w_in, gate_w2, gate_b, tm):
    n = x2d.shape[0]
    w_pad = jnp.pad(w_in, ((0, 0), (0, PROJ_PAD - w_in.shape[1]))).astype(BF16)
    w2_pad = jnp.pad(gate_w2, ((0, LANES - GLA_GATE_RANK), (0, 0)))
    gk = GLA_HEADS * GLA_DK
    return pl.pallas_call(
        _inproj_body,
        grid=(n // tm,),
        in_specs=[
            pl.BlockSpec((tm, D_MODEL), lambda i: (i, 0)),
            pl.BlockSpec((1, D_MODEL), lambda i: (0, 0)),
            pl.BlockSpec((D_MODEL, PROJ_PAD), lambda i: (0, 0)),
            pl.BlockSpec((LANES, gk), lambda i: (0, 0)),
            pl.BlockSpec((1, gk), lambda i: (0, 0)),
        ],
        out_specs=[
            pl.BlockSpec((tm, PROJ_W), lambda i: (i, 0)),
            pl.BlockSpec((tm, gk), lambda i: (i, 0)),
        ],
        out_shape=[
            jax.ShapeDtypeStruct((n, PROJ_W), F32),
            jax.ShapeDtypeStruct((n, gk), F32),
        ],
        compiler_params=_cparams(("parallel",)),
        name="inproj",
    )(x2d, norm_w.reshape(1, -1), w_pad, w2_pad, gate_b.reshape(1, -1))


def _half_norm(x, w_row):
    li = lax.broadcasted_iota(jnp.int32, (LANES, LANES), 0) // DIFF_D
    lj = lax.broadcasted_iota(jnp.int32, (LANES, LANES), 1) // DIFF_D
    grp = jnp.where(li == lj, 1.0 / DIFF_D, 0.0).astype(F32)
    ms = jnp.dot(x * x, grp, preferred_element_type=F32, precision=HIGHEST)
    return x * lax.rsqrt(ms + RMS_EPS) * w_row


def _attn_body(lq1_ref, lk1_ref, lq2_ref, lk2_ref, qnw_ref, knw_ref, onw_ref,
               q_ref, k_ref, v_ref, o_ref, kn_s, vb_s, m_s, l_s, acc_s,
               *, lam_init, tq, prep_rows):
    qi = pl.program_id(2)
    seq = k_ref.shape[0]

    @pl.when(qi == 0)
    def _prep():
        def body(c, carry):
            r = pl.ds(pl.multiple_of(c * prep_rows, prep_rows), prep_rows)
            kn_s[r, :] = _half_norm(k_ref[r, :], knw_ref[...]).astype(BF16)
            vb_s[r, :] = v_ref[r, :].astype(BF16)
            return carry
        lax.fori_loop(0, seq // prep_rows, body, 0)

    qn = _half_norm(q_ref[...], qnw_ref[...]) * (DIFF_D ** -0.5)
    lane = lax.broadcasted_iota(jnp.int32, (1, LANES), 1)
    q_maps = (jnp.where(lane < DIFF_D, qn, 0.0).astype(BF16),
              jnp.where(lane >= DIFF_D, qn, 0.0).astype(BF16))

    m_s[...] = jnp.full(m_s.shape, -jnp.inf, F32)
    l_s[...] = jnp.zeros(l_s.shape, F32)
    acc_s[...] = jnp.zeros(acc_s.shape, F32)

    def step(j, masked):
        r = pl.ds(pl.multiple_of(j * tq, tq), tq)
        kb = kn_s[r, :]
        vb = vb_s[r, :]
        if masked:
            row = lax.broadcasted_iota(jnp.int32, (tq, tq), 0)
            col = lax.broadcasted_iota(jnp.int32, (tq, tq), 1)
            keep = col <= row
        for i in range(2):
            s = lax.dot_general(q_maps[i], kb, NT_DIMS, preferred_element_type=F32)
            if masked:
                s = jnp.where(keep, s, -jnp.inf)
            m_prev = m_s[i]
            m_new = jnp.maximum(m_prev, jnp.max(s, axis=1, keepdims=True))
            alpha = jnp.exp(m_prev - m_new)
            p = jnp.exp(s - m_new)
            l_s[i] = alpha * l_s[i] + jnp.sum(p, axis=1, keepdims=True)
            acc_s[i] = alpha * acc_s[i] + jnp.dot(p.astype(BF16), vb,
                                                  preferred_element_type=F32)
            m_s[i] = m_new

    def off_diag(j, carry):
        step(j, False)
        return carry
    lax.fori_loop(0, qi, off_diag, 0)
    step(qi, True)

    lam = (jnp.exp(jnp.sum(lq1_ref[...] * lk1_ref[...], axis=-1, keepdims=True))
           - jnp.exp(jnp.sum(lq2_ref[...] * lk2_ref[...], axis=-1, keepdims=True))
           + lam_init)
    o = acc_s[0] / l_s[0] - lam * (acc_s[1] / l_s[1])
    o_ref[...] = _rms(o, onw_ref[...]) * (1.0 - lam_init)


def _diff_attention(proj, lq1, lk1, lq2, lk2, qnw, knw, onw, *, batch, seq, lam_init, tq):
    n = batch * seq
    nq = seq // tq
    prep_rows = min(512, seq)
    vec = lambda a: a.reshape(1, -1)
    tile2 = lambda a: jnp.concatenate([a, a]).reshape(1, -1)
    small = lambda w: pl.BlockSpec((1, w), lambda b, h, i: (0, 0))
    body = functools.partial(_attn_body, lam_init=lam_init, tq=tq, prep_rows=prep_rows)
    return pl.pallas_call(
        body,
        grid=(batch, DIFF_HEADS, nq),
        in_specs=[
            small(DIFF_D), small(DIFF_D), small(DIFF_D), small(DIFF_D),
            small(LANES), small(LANES), small(DIFF_DV),
            pl.BlockSpec((tq, LANES), lambda b, h, i: (b * nq + i, DQ_BLK + h)),
            pl.BlockSpec((seq, LANES), lambda b, h, i: (b, DK_BLK + h)),
            pl.BlockSpec((seq, LANES), lambda b, h, i: (b, DV_BLK + h)),
        ],
        out_specs=pl.BlockSpec((tq, DIFF_DV), lambda b, h, i: (b * nq + i, h)),
        out_shape=jax.ShapeDtypeStruct((n, DIFF_HEADS * DIFF_DV), F32),
        scratch_shapes=[
            pltpu.VMEM((seq, LANES), BF16),
            pltpu.VMEM((seq, DIFF_DV), BF16),
            pltpu.VMEM((2, tq, 1), F32),
            pltpu.VMEM((2, tq, 1), F32),
            pltpu.VMEM((2, tq, DIFF_DV), F32),
        ],
        compiler_params=_cparams(("parallel", "parallel", "arbitrary")),
        name="diff_attention",
    )(vec(lq1), vec(lk1), vec(lq2), vec(lk2), tile2(qnw), tile2(knw), vec(onw),
      proj, proj, proj)


def _gla_body(q_ref, k_ref, v_ref, go_ref, g_ref, onw_ref, o_ref, st_s, *, rows):
    @pl.when(pl.program_id(1) == 0)
    def _init():
        st_s[...] = jnp.zeros(st_s.shape, F32)

    n_chunks = rows // GLA_CHUNK
    row = lax.broadcasted_iota(jnp.int32, (rows, rows), 0)
    col = lax.broadcasted_iota(jnp.int32, (rows, rows), 1)
    same_chunk = (row // GLA_CHUNK) == (col // GLA_CHUNK)
    causal = jnp.logical_and(same_chunk, col <= row)
    cum_mat = jnp.where(causal, 1.0, 0.0).astype(F32)
    tot_mat = jnp.where(same_chunk, 1.0, 0.0).astype(F32)
    lane = lax.broadcasted_iota(jnp.int32, (1, LANES), 1)

    for pair in range(GLA_HEADS // 2):
        cs = slice(pair * LANES, (pair + 1) * LANES)
        g = g_ref[:, cs]
        b = jnp.dot(cum_mat, g, preferred_element_type=F32, precision=HIGHEST)
        b_last = jnp.dot(tot_mat, g, preferred_element_type=F32, precision=HIGHEST)
        kk = k_ref[:, cs]
        q_dec = q_ref[:, cs] * (GLA_DK ** -0.5) * jnp.exp(b)
        k_inv = (kk * jnp.exp(-b)).astype(BF16)
        k_dec = kk * jnp.exp(b_last - b)
        decay = jnp.exp(b_last)
        for sub in range(2):
            head = 2 * pair + sub
            in_head = (lane >= sub * GLA_DK) & (lane < (sub + 1) * GLA_DK)
            qd = jnp.where(in_head, q_dec, 0.0).astype(BF16)
            kd = jnp.where(in_head, k_dec, 0.0).astype(BF16)
            hs = slice(head * GLA_DV, (head + 1) * GLA_DV)
            vh = v_ref[:, hs].astype(BF16)
            a = lax.dot_general(qd, k_inv, NT_DIMS, preferred_element_type=F32)
            a = jnp.where(causal, a, 0.0).astype(BF16)
            o_intra = jnp.dot(a, vh, preferred_element_type=F32)
            st = st_s[head]
            for c in range(n_chunks):
                rs = slice(c * GLA_CHUNK, (c + 1) * GLA_CHUNK)
                o_c = o_intra[rs] + lax.dot_general(qd[rs], st.astype(BF16), NT_DIMS,
                                                    preferred_element_type=F32)
                upd = lax.dot_general(vh[rs], kd[rs], TN_DIMS, preferred_element_type=F32)
                st = decay[c * GLA_CHUNK:c * GLA_CHUNK + 1, :] * st + upd
                gate = go_ref[rs, hs]
                silu = gate / (1.0 + jnp.exp(-gate))
                o_ref[rs, hs] = _rms(o_c, onw_ref[...]) * silu
            st_s[head] = st


def _gla(proj, glog, onw, *, batch, seq, rows):
    n = batch * seq
    nr = seq // rows
    gk = GLA_HEADS * GLA_DK
    gv = GLA_HEADS * GLA_DV
    body = functools.partial(_gla_body, rows=rows)
    return pl.pallas_call(
        body,
        grid=(batch, nr),
        in_specs=[
            pl.BlockSpec((rows, gk), lambda b, r: (b * nr + r, GQ_BLK)),
            pl.BlockSpec((rows, gk), lambda b, r: (b * nr + r, GK_BLK)),
            pl.BlockSpec((rows, gv), lambda b, r: (b * nr + r, GV_BLK)),
            pl.BlockSpec((rows, gv), lambda b, r: (b * nr + r, GO_BLK)),
            pl.BlockSpec((rows, gk), lambda b, r: (b * nr + r, 0)),
            pl.BlockSpec((1, GLA_DV), lambda b, r: (0, 0)),
        ],
        out_specs=pl.BlockSpec((rows, gv), lambda b, r: (b * nr + r, 0)),
        out_shape=jax.ShapeDtypeStruct((n, gv), F32),
        scratch_shapes=[pltpu.VMEM((GLA_HEADS, GLA_DV, LANES), F32)],
        compiler_params=_cparams(("parallel", "arbitrary")),
        name="gla",
    )(proj, proj, proj, proj, glog, onw.reshape(1, -1))


def _top_rows(s, payload=None):
    n = s.shape[0]
    rid = lax.broadcasted_iota(jnp.int32, s.shape, 0).astype(F32)
    out_row = lax.broadcasted_iota(jnp.int32, (PEER_TOPK, s.shape[1]), 0)
    vals = jnp.zeros((PEER_TOPK, s.shape[1]), F32)
    picks = jnp.zeros((PEER_TOPK, s.shape[1]), F32)
    for r in range(PEER_TOPK):
        m = jnp.max(s, axis=0, keepdims=True)
        pos = jnp.min(jnp.where(s == m, rid, float(n)), axis=0, keepdims=True)
        hit = rid == pos
        pick = pos if payload is None else jnp.sum(
            jnp.where(hit, payload, 0.0), axis=0, keepdims=True)
        vals = jnp.where(out_row == r, m, vals)
        picks = jnp.where(out_row == r, pick, picks)
        s = jnp.where(hit, -jnp.inf, s)
    return vals, picks


def _route_body(do_ref, go_ref, x_ref, wo_ref, fnw_ref, wq_ref, keys_ref,
                x1_ref, h2_ref, idx_ref, gate_ref, q_s, ts_s, ti_s):
    half = DIFF_HEADS * DIFF_DV
    y = jnp.dot(do_ref[...].astype(BF16), wo_ref[:half, :], preferred_element_type=F32)
    y = y + jnp.dot(go_ref[...].astype(BF16), wo_ref[half:, :], preferred_element_type=F32)
    x1 = x_ref[...] + y
    x1_ref[...] = x1
    h2 = _rms(x1, fnw_ref[...])
    h2_ref[...] = h2
    q = jnp.dot(h2.astype(BF16), wq_ref[...], preferred_element_type=F32)
    n_groups = 2 * PEER_HEADS
    for g in range(n_groups):
        q_s[g] = q[:, g * LANES:(g + 1) * LANES].astype(BF16)

    def stage1(g, carry):
        s = lax.dot_general(keys_ref[g], q_s[g], NT_DIMS, preferred_element_type=F32)
        vals, rows = _top_rows(s)
        ts_s[g] = vals
        ti_s[g] = rows
        return carry
    lax.fori_loop(0, n_groups, stage1, 0)

    def stage2(hh, carry):
        s_a, s_b = ts_s[2 * hh], ts_s[2 * hh + 1]
        i_a, i_b = ti_s[2 * hh], ti_s[2 * hh + 1]
        cand = jnp.concatenate([s_a[i:i + 1, :] + s_b for i in range(PEER_TOPK)], axis=0)
        cidx = jnp.concatenate([i_a[i:i + 1, :] * float(PEER_N_KEYS) + i_b
                                for i in range(PEER_TOPK)], axis=0)
        best, expert = _top_rows(cand, cidx)
        e = jnp.exp(best - jnp.max(best, axis=0, keepdims=True))
        gate_ref[hh] = e / jnp.sum(e, axis=0, keepdims=True)
        idx_ref[hh] = expert.astype(jnp.int32)
        return carry
    lax.fori_loop(0, PEER_HEADS, stage2, 0)


def _route(diff_o, gla_o, x2d, w_out, ffn_nw, w_q, sub_keys, tm):
    n = x2d.shape[0]
    n_groups = 2 * PEER_HEADS
    keys = sub_keys.reshape(n_groups, PEER_N_KEYS, LANES).astype(BF16)
    half = DIFF_HEADS * DIFF_DV
    return pl.pallas_call(
        _route_body,
        grid=(n // tm,),
        in_specs=[
            pl.BlockSpec((tm, half), lambda i: (i, 0)),
            pl.BlockSpec((tm, half), lambda i: (i, 0)),
            pl.BlockSpec((tm, D_MODEL), lambda i: (i, 0)),
            pl.BlockSpec((D_MODEL, D_MODEL), lambda i: (0, 0)),
            pl.BlockSpec((1, D_MODEL), lambda i: (0, 0)),
            pl.BlockSpec((D_MODEL, n_groups * LANES), lambda i: (0, 0)),
            pl.BlockSpec((n_groups, PEER_N_KEYS, LANES), lambda i: (0, 0, 0)),
        ],
        out_specs=[
            pl.BlockSpec((tm, D_MODEL), lambda i: (i, 0)),
            pl.BlockSpec((tm, D_MODEL), lambda i: (i, 0)),
            pl.BlockSpec((PEER_HEADS, PEER_TOPK, tm), lambda i: (0, 0, i)),
            pl.BlockSpec((PEER_HEADS, PEER_TOPK, tm), lambda i: (0, 0, i)),
        ],
        out_shape=[
            jax.ShapeDtypeStruct((n, D_MODEL), F32),
            jax.ShapeDtypeStruct((n, D_MODEL), F32),
            jax.ShapeDtypeStruct((PEER_HEADS, PEER_TOPK, n), jnp.int32),
            jax.ShapeDtypeStruct((PEER_HEADS, PEER_TOPK, n), F32),
        ],
        scratch_shapes=[
            pltpu.VMEM((n_groups, tm, LANES), BF16),
            pltpu.VMEM((n_groups, PEER_TOPK, tm), F32),
            pltpu.VMEM((n_groups, PEER_TOPK, tm), F32),
        ],
        compiler_params=_cparams(("parallel",)),
        name="peer_route",
    )(diff_o, gla_o, x2d, w_out.astype(BF16), ffn_nw.reshape(1, -1), w_q.astype(BF16), keys)


def _peer_body(idx_ref, h_ref, x1_ref, gate_ref, tab_ref, o_ref, buf, sem, *, tb, grp):
    n_grp = tb // grp
    grp_rows = grp * PEER_SLOTS

    def issue(g, slot):
        def tok(tt, carry):
            t = g * grp + tt
            base = tt * PEER_SLOTS
            for j in range(PEER_SLOTS):
                e = idx_ref[j, t]
                pltpu.make_async_copy(
                    tab_ref.at[pl.ds(e, 1), :],
                    buf.at[slot, pl.ds(base + j, 1), :],
                    sem.at[slot],
                ).start(priority=j % 2)
            return carry
        lax.fori_loop(0, grp, tok, 0)

    def wait(slot):
        pltpu.make_async_copy(tab_ref.at[pl.ds(0, grp_rows), :], buf.at[slot],
                              sem.at[slot]).wait()

    def compute(g, slot):
        t0 = pl.multiple_of(g * grp, grp)
        hg = h_ref[pl.ds(t0, grp), :]
        lane_g = lax.broadcasted_iota(jnp.int32, (PEER_SLOTS, grp), 1)
        act = jnp.zeros((PEER_SLOTS, grp), F32)
        for tt in range(grp):
            u = buf[slot, tt * PEER_SLOTS:(tt + 1) * PEER_SLOTS, :D_MODEL]
            prod = u * hg[tt:tt + 1, :]
            part = prod[:, :LANES]
            for c in range(1, D_MODEL // LANES):
                part = part + prod[:, c * LANES:(c + 1) * LANES]
            act = jnp.where(lane_g == tt, jnp.sum(part, axis=1, keepdims=True), act)
        gelu = 0.5 * act * (1.0 + lax.erf(act * (2.0 ** -0.5)))
        w = gate_ref[g] * gelu
        sub_g = lax.broadcasted_iota(jnp.int32, (grp, D_MODEL), 0)
        out = jnp.zeros((grp, D_MODEL), F32)
        for tt in range(grp):
            v = buf[slot, tt * PEER_SLOTS:(tt + 1) * PEER_SLOTS, D_MODEL:]
            r = jnp.sum(v * w[:, tt:tt + 1], axis=0, keepdims=True)
            out = jnp.where(sub_g == tt, r, out)
        o_ref[pl.ds(t0, grp), :] = x1_ref[pl.ds(t0, grp), :] + out

    issue(0, 0)

    def body(g, carry):
        slot = lax.rem(g, 2)

        @pl.when(g + 1 < n_grp)
        def _next():
            issue(g + 1, 1 - slot)

        wait(slot)
        compute(g, slot)
        return carry
    lax.fori_loop(0, n_grp, body, 0)


def _peer(idx, h2, x1, gate, table, *, tb, grp):
    n = h2.shape[0]
    body = functools.partial(_peer_body, tb=tb, grp=grp)
    return pl.pallas_call(
        body,
        grid=(n // tb,),
        in_specs=[
            pl.BlockSpec((PEER_SLOTS, tb), lambda i: (0, i), memory_space=pltpu.SMEM),
            pl.BlockSpec((tb, D_MODEL), lambda i: (i, 0)),
            pl.BlockSpec((tb, D_MODEL), lambda i: (i, 0)),
            pl.BlockSpec((tb // grp, PEER_SLOTS, grp), lambda i: (i, 0, 0)),
            pl.BlockSpec(memory_space=pl.ANY),
        ],
        out_specs=pl.BlockSpec((tb, D_MODEL), lambda i: (i, 0)),
        out_shape=jax.ShapeDtypeStruct((n, D_MODEL), F32),
        scratch_shapes=[
            pltpu.VMEM((2, grp * PEER_SLOTS, 2 * D_MODEL), F32),
            pltpu.SemaphoreType.DMA((2,)),
        ],
        compiler_params=_cparams(("arbitrary",)),
        name="peer_gather",
    )(idx, h2, x1, gate, table)


def _tiles(batch, seq):
    n = batch * seq
    return dict(
        tm_proj=min(256, n),
        tq=min(512, seq),
        gla_rows=min(256, seq),
        tm_route=min(128, n),
        tb_peer=min(128, n),
        grp=8,
    )


def kernel(x, attn_norm_w, w_in, diff_q_norm_w, diff_k_norm_w, lambda_q1, lambda_k1, lambda_q2, lambda_k2, diff_out_norm_w, gla_gate_w2, gla_gate_b, gla_out_norm_w, w_out, ffn_norm_w, peer_w_q, peer_sub_keys, peer_u, peer_v):
    batch, seq, _ = x.shape
    n = batch * seq
    t = _tiles(batch, seq)
    depth = w_in.shape[0]
    x2d = x.reshape(n, D_MODEL)
    for l in range(depth):
        lam_init = 0.8 - 0.6 * math.exp(-0.3 * l)
        proj, glog = _inproj(x2d, attn_norm_w[l], w_in[l], gla_gate_w2[l], gla_gate_b[l],
                             t["tm_proj"])
        diff_o = _diff_attention(
            proj, lambda_q1[l], lambda_k1[l], lambda_q2[l], lambda_k2[l],
            diff_q_norm_w[l], diff_k_norm_w[l], diff_out_norm_w[l],
            batch=batch, seq=seq, lam_init=lam_init, tq=t["tq"])
        gla_o = _gla(proj, glog, gla_out_norm_w[l], batch=batch, seq=seq, rows=t["gla_rows"])
        x1, h2, idx, gate = _route(diff_o, gla_o, x2d, w_out[l], ffn_norm_w[l], peer_w_q[l],
                                   peer_sub_keys[l], t["tm_route"])
        grp = t["grp"]
        idx2d = idx.reshape(PEER_SLOTS, n)
        gate_g = gate.reshape(PEER_SLOTS, n // grp, grp).transpose(1, 0, 2)
        table = jnp.concatenate([peer_u[l], peer_v[l]], axis=1)
        x2d = _peer(idx2d, h2, x1, gate_g, table, tb=t["tb_peer"], grp=grp)
    return x2d.reshape(batch, seq, D_MODEL)
```

```python
import functools
import math

import jax
import jax.numpy as jnp
from jax import lax
from jax.experimental import pallas as pl
from jax.experimental.pallas import tpu as pltpu

F32 = jnp.float32
BF16 = jnp.bfloat16
HIGHEST = lax.Precision.HIGHEST

D_MODEL = 1024
DIFF_HEADS = 4
DIFF_D = 64
DIFF_DV = 128
GLA_HEADS = 4
GLA_DK = 64
GLA_DV = 128
GLA_GATE_RANK = 16
GLA_GATE_NORMALIZER = 16.0
GLA_CHUNK = 64
PEER_HEADS = 8
PEER_N_KEYS = 128
PEER_TOPK = 16
PEER_SLOTS = PEER_HEADS * PEER_TOPK
RMS_EPS = 1e-6

LANES = 128
PROJ_W = 3072
PROJ_PAD = PROJ_W + LANES

DQ_BLK, DK_BLK, DV_BLK = 0, 4, 8
GQ_BLK, GK_BLK = 6, 7
GV_BLK, GO_BLK = 4, 5

VMEM_LIMIT = 56 * 1024 * 1024

NT_DIMS = (((1,), (1,)), ((), ()))
TN_DIMS = (((0,), (0,)), ((), ()))


def _cparams(sem):
    return pltpu.CompilerParams(dimension_semantics=sem, vmem_limit_bytes=VMEM_LIMIT)


def _rms(x, w):
    ms = jnp.mean(x * x, axis=-1, keepdims=True)
    return x * lax.rsqrt(ms + RMS_EPS) * w


def _inproj_body(x_ref, nw_ref, w_ref, w2_ref, b_ref, proj_ref, glog_ref):
    h = _rms(x_ref[...], nw_ref[...])
    p = jnp.dot(h.astype(BF16), w_ref[...], preferred_element_type=F32)
    proj_ref[...] = p[:, :PROJ_W]
    g = jnp.dot(p[:, PROJ_W:], w2_ref[...], preferred_element_type=F32,
                precision=HIGHEST) + b_ref[...]
    log_sig = -(jnp.maximum(-g, 0.0) + jnp.log1p(jnp.exp(-jnp.abs(g))))
    glog_ref[...] = log_sig * (1.0 / GLA_GATE_NORMALIZER)


def _inproj(x2d, norm_w, w_in, gate_w2, gate_b, tm):
    n = x2d.shape[0]
    w_pad = jnp.pad(w_in, ((0, 0), (0, PROJ_PAD - w_in.shape[1]))).astype(BF16)
    w2_pad = jnp.pad(gate_w2, ((0, LANES - GLA_GATE_RANK), (0, 0)))
    gk = GLA_HEADS * GLA_DK
    return pl.pallas_call(
        _inproj_body,
        grid=(n // tm,),
        in_specs=[
            pl.BlockSpec((tm, D_MODEL), lambda i: (i, 0)),
            pl.BlockSpec((1, D_MODEL), lambda i: (0, 0)),
            pl.BlockSpec((D_MODEL, PROJ_PAD), lambda i: (0, 0)),
            pl.BlockSpec((LANES, gk), lambda i: (0, 0)),
            pl.BlockSpec((1, gk), lambda i: (0, 0)),
        ],
        out_specs=[
            pl.BlockSpec((tm, PROJ_W), lambda i: (i, 0)),
            pl.BlockSpec((tm, gk), lambda i: (i, 0)),
        ],
        out_shape=[
            jax.ShapeDtypeStruct((n, PROJ_W), F32),
            jax.ShapeDtypeStruct((n, gk), F32),
        ],
        compiler_params=_cparams(("parallel",)),
        name="inproj",
    )(x2d, norm_w.reshape(1, -1), w_pad, w2_pad, gate_b.reshape(1, -1))


def _half_norm(x, w_row):
    li = lax.broadcasted_iota(jnp.int32, (LANES, LANES), 0) // DIFF_D
    lj = lax.broadcasted_iota(jnp.int32, (LANES, LANES), 1) // DIFF_D
    grp = jnp.where(li == lj, 1.0 / DIFF_D, 0.0).astype(F32)
    ms = jnp.dot(x * x, grp, preferred_element_type=F32, precision=HIGHEST)
    return x * lax.rsqrt(ms + RMS_EPS) * w_row


def _attn_body(lq1_ref, lk1_ref, lq2_ref, lk2_ref, qnw_ref, knw_ref, onw_ref,
               q_ref, k_ref, v_ref, o_ref, kn_s, vt_s, m_s, l_s, acc_s,
               *, lam_init, tq, prep_rows):
    qi = pl.program_id(2)
    seq = k_ref.shape[0]

    @pl.when(qi == 0)
    def _prep():
        for c in range(seq // prep_rows):
            r = slice(c * prep_rows, (c + 1) * prep_rows)
            kn_s[r, :] = _half_norm(k_ref[r, :], knw_ref[...]).astype(BF16)
            vt_s[:, r] = v_ref[r, :].T.astype(BF16)

    qn = _half_norm(q_ref[...], qnw_ref[...]) * (DIFF_D ** -0.5 * math.log2(math.e))
    lane = lax.broadcasted_iota(jnp.int32, (1, LANES), 1)
    qs = jnp.concatenate([jnp.where(lane < DIFF_D, qn, 0.0),
                          jnp.where(lane >= DIFF_D, qn, 0.0)], axis=0).astype(BF16)

    m_s[...] = jnp.full(m_s.shape, -jnp.inf, F32)
    l_s[...] = jnp.zeros(l_s.shape, F32)
    acc_s[...] = jnp.zeros(acc_s.shape, F32)

    def step(j, masked):
        r = pl.ds(pl.multiple_of(j * tq, tq), tq)
        s = lax.dot_general(kn_s[r, :], qs, NT_DIMS, preferred_element_type=F32)
        if masked:
            kpos = lax.broadcasted_iota(jnp.int32, s.shape, 0)
            qpos = lax.broadcasted_iota(jnp.int32, s.shape, 1) & (tq - 1)
            s = jnp.where(kpos <= qpos, s, -jnp.inf)
        m_prev = m_s[...]
        m_new = jnp.maximum(m_prev, jnp.max(s, axis=0, keepdims=True))
        alpha = jnp.exp2(m_prev - m_new)
        p = jnp.exp2(s - m_new)
        l_s[...] = alpha * l_s[...] + jnp.sum(p, axis=0, keepdims=True)
        acc_s[...] = alpha * acc_s[...] + jnp.dot(vt_s[:, r], p.astype(BF16),
                                                  preferred_element_type=F32)
        m_s[...] = m_new

    def off_diag(j, carry):
        step(j, False)
        return carry
    lax.fori_loop(0, qi, off_diag, 0)
    step(qi, True)

    lam = (jnp.exp(jnp.sum(lq1_ref[...] * lk1_ref[...], axis=-1, keepdims=True))
           - jnp.exp(jnp.sum(lq2_ref[...] * lk2_ref[...], axis=-1, keepdims=True))
           + lam_init)
    o = acc_s[...] / l_s[...]
    o = o[:, :tq] - lam * o[:, tq:]
    ms = jnp.mean(o * o, axis=0, keepdims=True)
    o = o * lax.rsqrt(ms + RMS_EPS) * onw_ref[...] * (1.0 - lam_init)
    o_ref[...] = o.T


def _diff_attention(proj, lq1, lk1, lq2, lk2, qnw, knw, onw, *, batch, seq, lam_init, tq):
    assert tq & (tq - 1) == 0, "the causal mask uses tq as a power of two"
    n = batch * seq
    nq = seq // tq
    prep_rows = min(512, seq)
    vec = lambda a: a.reshape(1, -1)
    tile2 = lambda a: jnp.concatenate([a, a]).reshape(1, -1)
    small = lambda w: pl.BlockSpec((1, w), lambda b, h, i: (0, 0))
    body = functools.partial(_attn_body, lam_init=lam_init, tq=tq, prep_rows=prep_rows)
    onw_cols = jnp.broadcast_to(onw.reshape(-1, 1), (DIFF_DV, tq))
    return pl.pallas_call(
        body,
        grid=(batch, DIFF_HEADS, nq),
        in_specs=[
            small(DIFF_D), small(DIFF_D), small(DIFF_D), small(DIFF_D),
            small(LANES), small(LANES),
            pl.BlockSpec((DIFF_DV, tq), lambda b, h, i: (0, 0)),
            pl.BlockSpec((tq, LANES), lambda b, h, i: (b * nq + i, DQ_BLK + h)),
            pl.BlockSpec((seq, LANES), lambda b, h, i: (b, DK_BLK + h)),
            pl.BlockSpec((seq, LANES), lambda b, h, i: (b, DV_BLK + h)),
        ],
        out_specs=pl.BlockSpec((tq, DIFF_DV), lambda b, h, i: (b * nq + i, h)),
        out_shape=jax.ShapeDtypeStruct((n, DIFF_HEADS * DIFF_DV), F32),
        scratch_shapes=[
            pltpu.VMEM((seq, LANES), BF16),
            pltpu.VMEM((DIFF_DV, seq), BF16),
            pltpu.VMEM((1, 2 * tq), F32),
            pltpu.VMEM((1, 2 * tq), F32),
            pltpu.VMEM((DIFF_DV, 2 * tq), F32),
        ],
        compiler_params=_cparams(("parallel", "parallel", "arbitrary")),
        name="diff_attention",
    )(vec(lq1), vec(lk1), vec(lq2), vec(lk2), tile2(qnw), tile2(knw), onw_cols,
      proj, proj, proj)


def _gla_body(q_ref, k_ref, v_ref, go_ref, g_ref, onw_ref, o_ref, st_s, *, rows):
    @pl.when(pl.program_id(1) == 0)
    def _init():
        st_s[...] = jnp.zeros(st_s.shape, F32)

    n_chunks = rows // GLA_CHUNK
    row = lax.broadcasted_iota(jnp.int32, (rows, rows), 0)
    col = lax.broadcasted_iota(jnp.int32, (rows, rows), 1)
    same_chunk = (row // GLA_CHUNK) == (col // GLA_CHUNK)
    causal = jnp.logical_and(same_chunk, col <= row)
    cum_mat = jnp.where(causal, 1.0, 0.0).astype(F32)
    tot_mat = jnp.where(same_chunk, 1.0, 0.0).astype(F32)
    lane = lax.broadcasted_iota(jnp.int32, (1, LANES), 1)

    for pair in range(GLA_HEADS // 2):
        cs = slice(pair * LANES, (pair + 1) * LANES)
        g = g_ref[:, cs]
        b = jnp.dot(cum_mat, g, preferred_element_type=F32, precision=HIGHEST)
        b_last = jnp.dot(tot_mat, g, preferred_element_type=F32, precision=HIGHEST)
        kk = k_ref[:, cs]
        q_dec = q_ref[:, cs] * (GLA_DK ** -0.5) * jnp.exp(b)
        k_inv = (kk * jnp.exp(-b)).astype(BF16)
        k_dec = kk * jnp.exp(b_last - b)
        decay = jnp.exp(b_last)
        for sub in range(2):
            head = 2 * pair + sub
            in_head = (lane >= sub * GLA_DK) & (lane < (sub + 1) * GLA_DK)
            qd = jnp.where(in_head, q_dec, 0.0).astype(BF16)
            kd = jnp.where(in_head, k_dec, 0.0).astype(BF16)
            hs = slice(head * GLA_DV, (head + 1) * GLA_DV)
            vh = v_ref[:, hs].astype(BF16)
            a = lax.dot_general(qd, k_inv, NT_DIMS, preferred_element_type=F32)
            a = jnp.where(causal, a, 0.0).astype(BF16)
            o_intra = jnp.dot(a, vh, preferred_element_type=F32)
            st = st_s[head]
            for c in range(n_chunks):
                rs = slice(c * GLA_CHUNK, (c + 1) * GLA_CHUNK)
                o_c = o_intra[rs] + lax.dot_general(qd[rs], st.astype(BF16), NT_DIMS,
                                                    preferred_element_type=F32)
                upd = lax.dot_general(vh[rs], kd[rs], TN_DIMS, preferred_element_type=F32)
                st = decay[c * GLA_CHUNK:c * GLA_CHUNK + 1, :] * st + upd
                gate = go_ref[rs, hs]
                silu = gate / (1.0 + jnp.exp(-gate))
                o_ref[rs, hs] = _rms(o_c, onw_ref[...]) * silu
            st_s[head] = st


def _gla(proj, glog, onw, *, batch, seq, rows):
    n = batch * seq
    nr = seq // rows
    gk = GLA_HEADS * GLA_DK
    gv = GLA_HEADS * GLA_DV
    body = functools.partial(_gla_body, rows=rows)
    return pl.pallas_call(
        body,
        grid=(batch, nr),
        in_specs=[
            pl.BlockSpec((rows, gk), lambda b, r: (b * nr + r, GQ_BLK)),
            pl.BlockSpec((rows, gk), lambda b, r: (b * nr + r, GK_BLK)),
            pl.BlockSpec((rows, gv), lambda b, r: (b * nr + r, GV_BLK)),
            pl.BlockSpec((rows, gv), lambda b, r: (b * nr + r, GO_BLK)),
            pl.BlockSpec((rows, gk), lambda b, r: (b * nr + r, 0)),
            pl.BlockSpec((1, GLA_DV), lambda b, r: (0, 0)),
        ],
        out_specs=pl.BlockSpec((rows, gv), lambda b, r: (b * nr + r, 0)),
        out_shape=jax.ShapeDtypeStruct((n, gv), F32),
        scratch_shapes=[pltpu.VMEM((GLA_HEADS, GLA_DV, LANES), F32)],
        compiler_params=_cparams(("parallel", "arbitrary")),
        name="gla",
    )(proj, proj, proj, proj, glog, onw.reshape(1, -1))


def _top_rows(s, payload=None):
    n = s.shape[0]
    rid = lax.broadcasted_iota(jnp.int32, s.shape, 0).astype(F32)
    out_row = lax.broadcasted_iota(jnp.int32, (PEER_TOPK, s.shape[1]), 0)
    vals = jnp.zeros((PEER_TOPK, s.shape[1]), F32)
    picks = jnp.zeros((PEER_TOPK, s.shape[1]), F32)
    for r in range(PEER_TOPK):
        m = jnp.max(s, axis=0, keepdims=True)
        pos = jnp.min(jnp.where(s == m, rid, float(n)), axis=0, keepdims=True)
        hit = rid == pos
        pick = pos if payload is None else jnp.sum(
            jnp.where(hit, payload, 0.0), axis=0, keepdims=True)
        vals = jnp.where(out_row == r, m, vals)
        picks = jnp.where(out_row == r, pick, picks)
        s = jnp.where(hit, -jnp.inf, s)
    return vals, picks


def _product_candidates(s_a, i_a, s_b, i_b):
    k = PEER_TOPK
    sub = lax.broadcasted_iota(jnp.int32, (8, s_a.shape[1]), 0)
    cands = [s_a[0:1] + s_b]
    ids = [i_a[0:1] * float(PEER_N_KEYS) + i_b]
    for i in range(1, 8):
        keep = k // (i + 1)
        c = s_a[i:i + 1] + s_b[0:8]
        cands.append(c if keep >= 8 else jnp.where(sub < keep, c, -jnp.inf))
        ids.append(i_a[i:i + 1] * float(PEER_N_KEYS) + i_b[0:8])
    cands.append(s_a[8:k] + s_b[0:1])
    ids.append(i_a[8:k] * float(PEER_N_KEYS) + i_b[0:1])
    return jnp.concatenate(cands, axis=0), jnp.concatenate(ids, axis=0)


def _route_body(do_ref, go_ref, x_ref, wo_ref, fnw_ref, wq_ref, keys_ref,
                x1_ref, h2_ref, idx_ref, gate_ref, q_s):
    half = DIFF_HEADS * DIFF_DV
    y = jnp.dot(do_ref[...].astype(BF16), wo_ref[:half, :], preferred_element_type=F32)
    y = y + jnp.dot(go_ref[...].astype(BF16), wo_ref[half:, :], preferred_element_type=F32)
    x1 = x_ref[...] + y
    x1_ref[...] = x1
    h2 = _rms(x1, fnw_ref[...])
    h2_ref[...] = h2
    q = jnp.dot(h2.astype(BF16), wq_ref[...], preferred_element_type=F32)
    n_groups = 2 * PEER_HEADS
    for g in range(n_groups):
        q_s[g] = q[:, g * LANES:(g + 1) * LANES].astype(BF16)

    def head_pair(hp, carry):
        for sub in range(2):
            hh = 2 * hp + sub
            tops = []
            for c in range(2):
                g = 2 * hh + c
                s = lax.dot_general(keys_ref[g], q_s[g], NT_DIMS, preferred_element_type=F32)
                tops.append(_top_rows(s))
            (s_a, i_a), (s_b, i_b) = tops
            cand, cidx = _product_candidates(s_a, i_a, s_b, i_b)
            best, expert = _top_rows(cand, cidx)
            e = jnp.exp(best - jnp.max(best, axis=0, keepdims=True))
            gate_ref[hh] = e / jnp.sum(e, axis=0, keepdims=True)
            idx_ref[hh] = expert.astype(jnp.int32)
        return carry
    lax.fori_loop(0, PEER_HEADS // 2, head_pair, 0)


def _route(diff_o, gla_o, x2d, w_out, ffn_nw, w_q, sub_keys, tm):
    n = x2d.shape[0]
    n_groups = 2 * PEER_HEADS
    keys = sub_keys.reshape(n_groups, PEER_N_KEYS, LANES).astype(BF16)
    half = DIFF_HEADS * DIFF_DV
    return pl.pallas_call(
        _route_body,
        grid=(n // tm,),
        in_specs=[
            pl.BlockSpec((tm, half), lambda i: (i, 0)),
            pl.BlockSpec((tm, half), lambda i: (i, 0)),
            pl.BlockSpec((tm, D_MODEL), lambda i: (i, 0)),
            pl.BlockSpec((D_MODEL, D_MODEL), lambda i: (0, 0)),
            pl.BlockSpec((1, D_MODEL), lambda i: (0, 0)),
            pl.BlockSpec((D_MODEL, n_groups * LANES), lambda i: (0, 0)),
            pl.BlockSpec((n_groups, PEER_N_KEYS, LANES), lambda i: (0, 0, 0)),
        ],
        out_specs=[
            pl.BlockSpec((tm, D_MODEL), lambda i: (i, 0)),
            pl.BlockSpec((tm, D_MODEL), lambda i: (i, 0)),
            pl.BlockSpec((PEER_HEADS, PEER_TOPK, tm), lambda i: (0, 0, i)),
            pl.BlockSpec((PEER_HEADS, PEER_TOPK, tm), lambda i: (0, 0, i)),
        ],
        out_shape=[
            jax.ShapeDtypeStruct((n, D_MODEL), F32),
            jax.ShapeDtypeStruct((n, D_MODEL), F32),
            jax.ShapeDtypeStruct((PEER_HEADS, PEER_TOPK, n), jnp.int32),
            jax.ShapeDtypeStruct((PEER_HEADS, PEER_TOPK, n), F32),
        ],
        scratch_shapes=[pltpu.VMEM((n_groups, tm, LANES), BF16)],
        compiler_params=_cparams(("parallel",)),
        name="peer_route",
    )(diff_o, gla_o, x2d, w_out.astype(BF16), ffn_nw.reshape(1, -1), w_q.astype(BF16), keys)


SLAB = 2 * D_MODEL // LANES
U_ROWS = D_MODEL // LANES


def _peer_body(idx_ref, h_ref, x1_ref, gate_ref, tab_ref, o_ref, buf, sem, *, tb, grp, pitch):
    n_grp = tb // grp
    grp_slabs = grp * PEER_SLOTS

    def issue(g, slot):
        def tok(tt, carry):
            t = g * grp + tt
            slab0 = slot * grp_slabs + tt * PEER_SLOTS
            for j in range(PEER_SLOTS):
                e = idx_ref[t, j]
                pltpu.make_async_copy(
                    tab_ref.at[pl.ds(pl.multiple_of(e * SLAB, SLAB), SLAB), :],
                    buf.at[pl.ds(pl.multiple_of((slab0 + j) * pitch, 8), SLAB), :],
                    sem.at[slot],
                ).start(priority=j % 2)
            return carry
        lax.fori_loop(0, grp, tok, 0)

    def wait(slot):
        pltpu.make_async_copy(tab_ref.at[pl.ds(0, grp_slabs * SLAB), :],
                              buf.at[pl.ds(0, grp_slabs * SLAB), :], sem.at[slot]).wait()

    def compute(g, slot):
        t0 = pl.multiple_of(g * grp, grp)
        hg = h_ref[pl.ds(t0, grp), :]
        lane_g = lax.broadcasted_iota(jnp.int32, (PEER_SLOTS, grp), 1)
        act = jnp.zeros((PEER_SLOTS, grp), F32)
        for tt in range(grp):
            row0 = (slot * grp_slabs + tt * PEER_SLOTS) * pitch
            part = None
            for s in range(U_ROWS):
                us = buf[pl.ds(row0 + s, PEER_SLOTS, stride=pitch), :]
                term = us * hg[tt:tt + 1, s * LANES:(s + 1) * LANES]
                part = term if part is None else part + term
            act = jnp.where(lane_g == tt, jnp.sum(part, axis=1, keepdims=True), act)
        gelu = 0.5 * act * (1.0 + lax.erf(act * (2.0 ** -0.5)))
        w = gate_ref[g] * gelu
        sub_g = lax.broadcasted_iota(jnp.int32, (grp, D_MODEL), 0)
        out = jnp.zeros((grp, D_MODEL), F32)
        for tt in range(grp):
            row0 = (slot * grp_slabs + tt * PEER_SLOTS) * pitch
            wcol = w[:, tt:tt + 1]
            pieces = []
            for s in range(U_ROWS):
                vs = buf[pl.ds(row0 + U_ROWS + s, PEER_SLOTS, stride=pitch), :]
                pieces.append(jnp.sum(vs * wcol, axis=0, keepdims=True))
            out = jnp.where(sub_g == tt, jnp.concatenate(pieces, axis=1), out)
        o_ref[pl.ds(t0, grp), :] = x1_ref[pl.ds(t0, grp), :] + out

    issue(0, 0)

    def body(g, carry):
        slot = lax.rem(g, 2)

        @pl.when(g + 1 < n_grp)
        def _next():
            issue(g + 1, 1 - slot)

        wait(slot)
        compute(g, slot)
        return carry
    lax.fori_loop(0, n_grp, body, 0)


def _peer(idx, h2, x1, gate, table, *, tb, grp, pitch):
    n = h2.shape[0]
    body = functools.partial(_peer_body, tb=tb, grp=grp, pitch=pitch)
    return pl.pallas_call(
        body,
        grid=(n // tb,),
        in_specs=[
            pl.BlockSpec((tb, PEER_SLOTS), lambda i: (i, 0), memory_space=pltpu.SMEM),
            pl.BlockSpec((tb, D_MODEL), lambda i: (i, 0)),
            pl.BlockSpec((tb, D_MODEL), lambda i: (i, 0)),
            pl.BlockSpec((tb // grp, PEER_SLOTS, grp), lambda i: (i, 0, 0)),
            pl.BlockSpec(memory_space=pl.ANY),
        ],
        out_specs=pl.BlockSpec((tb, D_MODEL), lambda i: (i, 0)),
        out_shape=jax.ShapeDtypeStruct((n, D_MODEL), F32),
        scratch_shapes=[
            pltpu.VMEM((2 * grp * PEER_SLOTS * pitch, LANES), F32),
            pltpu.SemaphoreType.DMA((2,)),
        ],
        compiler_params=_cparams(("arbitrary",)),
        name="peer_gather",
    )(idx, h2, x1, gate, table)


def _expert_table(u_tab, v_tab):
    e = u_tab.shape[0]
    return jnp.concatenate([u_tab, v_tab], axis=1).reshape(e * SLAB, LANES)


def _tiles(batch, seq):
    n = batch * seq
    return dict(
        tm_proj=min(256, n),
        tq=min(512, seq),
        gla_rows=min(256, seq),
        tm_route=min(128, n),
        tb_peer=min(128, n),
        grp=8,
        pitch=24,
    )


def kernel(x, attn_norm_w, w_in, diff_q_norm_w, diff_k_norm_w, lambda_q1, lambda_k1, lambda_q2, lambda_k2, diff_out_norm_w, gla_gate_w2, gla_gate_b, gla_out_norm_w, w_out, ffn_norm_w, peer_w_q, peer_sub_keys, peer_u, peer_v):
    batch, seq, _ = x.shape
    n = batch * seq
    t = _tiles(batch, seq)
    depth = w_in.shape[0]
    x2d = x.reshape(n, D_MODEL)
    for l in range(depth):
        lam_init = 0.8 - 0.6 * math.exp(-0.3 * l)
        proj, glog = _inproj(x2d, attn_norm_w[l], w_in[l], gla_gate_w2[l], gla_gate_b[l],
                             t["tm_proj"])
        diff_o = _diff_attention(
            proj, lambda_q1[l], lambda_k1[l], lambda_q2[l], lambda_k2[l],
            diff_q_norm_w[l], diff_k_norm_w[l], diff_out_norm_w[l],
            batch=batch, seq=seq, lam_init=lam_init, tq=t["tq"])
        gla_o = _gla(proj, glog, gla_out_norm_w[l], batch=batch, seq=seq, rows=t["gla_rows"])
        x1, h2, idx, gate = _route(diff_o, gla_o, x2d, w_out[l], ffn_norm_w[l], peer_w_q[l],
                                   peer_sub_keys[l], t["tm_route"])
        grp = t["grp"]
        idx2d = idx.reshape(PEER_SLOTS, n).T
        gate_g = gate.reshape(PEER_SLOTS, n // grp, grp).transpose(1, 0, 2)
        table = _expert_table(peer_u[l], peer_v[l])
        x2d = _peer(idx2d, h2, x1, gate_g, table, tb=t["tb_peer"], grp=grp, pitch=t["pitch"])
    return x2d.reshape(batch, seq, D_MODEL)
```

```python
import functools
import math

import jax
import jax.numpy as jnp
from jax import lax
from jax.experimental import pallas as pl
from jax.experimental.pallas import tpu as pltpu

F32 = jnp.float32
BF16 = jnp.bfloat16
HIGHEST = lax.Precision.HIGHEST

D_MODEL = 1024
DIFF_HEADS = 4
DIFF_D = 64
DIFF_DV = 128
GLA_HEADS = 4
GLA_DK = 64
GLA_DV = 128
GLA_GATE_RANK = 16
GLA_GATE_NORMALIZER = 16.0
GLA_CHUNK = 64
PEER_HEADS = 8
PEER_N_KEYS = 128
PEER_TOPK = 16
PEER_SLOTS = PEER_HEADS * PEER_TOPK
RMS_EPS = 1e-6

LANES = 128
PROJ_W = 3072
PROJ_PAD = PROJ_W + LANES

DQ_BLK, DK_BLK, DV_BLK = 0, 4, 8
GQ_BLK, GK_BLK = 6, 7
GV_BLK, GO_BLK = 4, 5

VMEM_LIMIT = 56 * 1024 * 1024

NT_DIMS = (((1,), (1,)), ((), ()))
TN_DIMS = (((0,), (0,)), ((), ()))


def _cparams(sem):
    return pltpu.CompilerParams(dimension_semantics=sem, vmem_limit_bytes=VMEM_LIMIT)


def _rms(x, w):
    ms = jnp.mean(x * x, axis=-1, keepdims=True)
    return x * lax.rsqrt(ms + RMS_EPS) * w


def _inproj_body(x_ref, nw_ref, w_ref, w2_ref, b_ref, proj_ref, glog_ref):
    h = _rms(x_ref[...], nw_ref[...])
    p = jnp.dot(h.astype(BF16), w_ref[...], preferred_element_type=F32)
    proj_ref[...] = p[:, :PROJ_W]
    g = jnp.dot(p[:, PROJ_W:], w2_ref[...], preferred_element_type=F32,
                precision=HIGHEST) + b_ref[...]
    log_sig = -(jnp.maximum(-g, 0.0) + jnp.log1p(jnp.exp(-jnp.abs(g))))
    glog_ref[...] = log_sig * (1.0 / GLA_GATE_NORMALIZER)


def _inproj(x2d, norm_w, w_in, gate_w2, gate_b, tm):
    n = x2d.shape[0]
    w_pad = jnp.pad(w_in, ((0, 0), (0, PROJ_PAD - w_in.shape[1]))).astype(BF16)
    w2_pad = jnp.pad(gate_w2, ((0, LANES - GLA_GATE_RANK), (0, 0)))
    gk = GLA_HEADS * GLA_DK
    return pl.pallas_call(
        _inproj_body,
        grid=(n // tm,),
        in_specs=[
            pl.BlockSpec((tm, D_MODEL), lambda i: (i, 0)),
            pl.BlockSpec((1, D_MODEL), lambda i: (0, 0)),
            pl.BlockSpec((D_MODEL, PROJ_PAD), lambda i: (0, 0)),
            pl.BlockSpec((LANES, gk), lambda i: (0, 0)),
            pl.BlockSpec((1, gk), lambda i: (0, 0)),
        ],
        out_specs=[
            pl.BlockSpec((tm, PROJ_W), lambda i: (i, 0)),
            pl.BlockSpec((tm, gk), lambda i: (i, 0)),
        ],
        out_shape=[
            jax.ShapeDtypeStruct((n, PROJ_W), F32),
            jax.ShapeDtypeStruct((n, gk), F32),
        ],
        compiler_params=_cparams(("parallel",)),
        name="inproj",
    )(x2d, norm_w.reshape(1, -1), w_pad, w2_pad, gate_b.reshape(1, -1))


def _half_norm(x, w_row):
    li = lax.broadcasted_iota(jnp.int32, (LANES, LANES), 0) // DIFF_D
    lj = lax.broadcasted_iota(jnp.int32, (LANES, LANES), 1) // DIFF_D
    grp = jnp.where(li == lj, 1.0 / DIFF_D, 0.0).astype(F32)
    ms = jnp.dot(x * x, grp, preferred_element_type=F32, precision=HIGHEST)
    return x * lax.rsqrt(ms + RMS_EPS) * w_row


def _attn_body(lq1_ref, lk1_ref, lq2_ref, lk2_ref, qnw_ref, knw_ref, onw_ref,
               q_ref, k_ref, v_ref, o_ref, kn_s, vt_s, m_s, l_s, acc_s,
               *, lam_init, tq, tk, prep_rows):
    qi = pl.program_id(2)
    seq = k_ref.shape[0]

    @pl.when(qi == 0)
    def _prep():
        for c in range(seq // prep_rows):
            r = slice(c * prep_rows, (c + 1) * prep_rows)
            kn_s[r, :] = _half_norm(k_ref[r, :], knw_ref[...]).astype(BF16)
            vt_s[:, r] = v_ref[r, :].T.astype(BF16)

    qn = _half_norm(q_ref[...], qnw_ref[...]) * (DIFF_D ** -0.5 * math.log2(math.e))
    lane = lax.broadcasted_iota(jnp.int32, (1, LANES), 1)
    qs = jnp.concatenate([jnp.where(lane < DIFF_D, qn, 0.0),
                          jnp.where(lane >= DIFF_D, qn, 0.0)], axis=0).astype(BF16)

    m_s[...] = jnp.full(m_s.shape, -jnp.inf, F32)
    l_s[...] = jnp.zeros(l_s.shape, F32)
    acc_s[...] = jnp.zeros(acc_s.shape, F32)

    def step(j, diag):
        r = pl.ds(pl.multiple_of(j * tk, tk), tk)
        s = lax.dot_general(kn_s[r, :], qs, NT_DIMS, preferred_element_type=F32)
        if diag is not None:
            kpos = lax.broadcasted_iota(jnp.int32, s.shape, 0) + diag * tk
            qpos = lax.broadcasted_iota(jnp.int32, s.shape, 1) & (tq - 1)
            s = jnp.where(kpos <= qpos, s, -jnp.inf)
        m_prev = m_s[...]
        m_new = jnp.maximum(m_prev, jnp.max(s, axis=0, keepdims=True))
        alpha = jnp.exp2(m_prev - m_new)
        p = jnp.exp2(s - m_new)
        l_s[...] = alpha * l_s[...] + jnp.sum(p, axis=0, keepdims=True)
        acc_s[...] = alpha * acc_s[...] + jnp.dot(vt_s[:, r], p.astype(BF16),
                                                  preferred_element_type=F32)
        m_s[...] = m_new

    per_tile = tq // tk

    def off_diag(j, carry):
        step(j, None)
        return carry
    lax.fori_loop(0, qi * per_tile, off_diag, 0)
    for d in range(per_tile):
        step(qi * per_tile + d, d)

    lam = (jnp.exp(jnp.sum(lq1_ref[...] * lk1_ref[...], axis=-1, keepdims=True))
           - jnp.exp(jnp.sum(lq2_ref[...] * lk2_ref[...], axis=-1, keepdims=True))
           + lam_init)
    o = acc_s[...] / l_s[...]
    o = o[:, :tq] - lam * o[:, tq:]
    ms = jnp.mean(o * o, axis=0, keepdims=True)
    o = o * lax.rsqrt(ms + RMS_EPS) * onw_ref[...] * (1.0 - lam_init)
    o_ref[...] = o.T


def _diff_attention(proj, lq1, lk1, lq2, lk2, qnw, knw, onw, *, batch, seq, lam_init, tq, tk):
    assert tq & (tq - 1) == 0, "the causal mask uses tq as a power of two"
    n = batch * seq
    nq = seq // tq
    prep_rows = min(512, seq)
    vec = lambda a: a.reshape(1, -1)
    tile2 = lambda a: jnp.concatenate([a, a]).reshape(1, -1)
    small = lambda w: pl.BlockSpec((1, w), lambda b, h, i: (0, 0))
    body = functools.partial(_attn_body, lam_init=lam_init, tq=tq, tk=tk, prep_rows=prep_rows)
    onw_cols = jnp.broadcast_to(onw.reshape(-1, 1), (DIFF_DV, tq))
    return pl.pallas_call(
        body,
        grid=(batch, DIFF_HEADS, nq),
        in_specs=[
            small(DIFF_D), small(DIFF_D), small(DIFF_D), small(DIFF_D),
            small(LANES), small(LANES),
            pl.BlockSpec((DIFF_DV, tq), lambda b, h, i: (0, 0)),
            pl.BlockSpec((tq, LANES), lambda b, h, i: (b * nq + i, DQ_BLK + h)),
            pl.BlockSpec((seq, LANES), lambda b, h, i: (b, DK_BLK + h)),
            pl.BlockSpec((seq, LANES), lambda b, h, i: (b, DV_BLK + h)),
        ],
        out_specs=pl.BlockSpec((tq, DIFF_DV), lambda b, h, i: (b * nq + i, h)),
        out_shape=jax.ShapeDtypeStruct((n, DIFF_HEADS * DIFF_DV), F32),
        scratch_shapes=[
            pltpu.VMEM((seq, LANES), BF16),
            pltpu.VMEM((DIFF_DV, seq), BF16),
            pltpu.VMEM((1, 2 * tq), F32),
            pltpu.VMEM((1, 2 * tq), F32),
            pltpu.VMEM((DIFF_DV, 2 * tq), F32),
        ],
        compiler_params=_cparams(("parallel", "parallel", "arbitrary")),
        name="diff_attention",
    )(vec(lq1), vec(lk1), vec(lq2), vec(lk2), tile2(qnw), tile2(knw), onw_cols,
      proj, proj, proj)


def _gla_body(q_ref, k_ref, v_ref, go_ref, g_ref, onw_ref, o_ref, st_s, *, rows):
    @pl.when(pl.program_id(1) == 0)
    def _init():
        st_s[...] = jnp.zeros(st_s.shape, F32)

    n_chunks = rows // GLA_CHUNK
    row = lax.broadcasted_iota(jnp.int32, (rows, rows), 0)
    col = lax.broadcasted_iota(jnp.int32, (rows, rows), 1)
    same_chunk = (row // GLA_CHUNK) == (col // GLA_CHUNK)
    causal = jnp.logical_and(same_chunk, col <= row)
    cum_mat = jnp.where(causal, 1.0, 0.0).astype(F32)
    tot_mat = jnp.where(same_chunk, 1.0, 0.0).astype(F32)
    lane = lax.broadcasted_iota(jnp.int32, (1, LANES), 1)

    for pair in range(GLA_HEADS // 2):
        cs = slice(pair * LANES, (pair + 1) * LANES)
        g = g_ref[:, cs]
        b = jnp.dot(cum_mat, g, preferred_element_type=F32, precision=HIGHEST)
        b_last = jnp.dot(tot_mat, g, preferred_element_type=F32, precision=HIGHEST)
        kk = k_ref[:, cs]
        q_dec = q_ref[:, cs] * (GLA_DK ** -0.5) * jnp.exp(b)
        k_inv = (kk * jnp.exp(-b)).astype(BF16)
        k_dec = kk * jnp.exp(b_last - b)
        decay = jnp.exp(b_last)
        for sub in range(2):
            head = 2 * pair + sub
            in_head = (lane >= sub * GLA_DK) & (lane < (sub + 1) * GLA_DK)
            qd = jnp.where(in_head, q_dec, 0.0).astype(BF16)
            kd = jnp.where(in_head, k_dec, 0.0).astype(BF16)
            hs = slice(head * GLA_DV, (head + 1) * GLA_DV)
            vh = v_ref[:, hs].astype(BF16)
            a = lax.dot_general(qd, k_inv, NT_DIMS, preferred_element_type=F32)
            a = jnp.where(causal, a, 0.0).astype(BF16)
            o_intra = jnp.dot(a, vh, preferred_element_type=F32)
            st = st_s[head]
            for c in range(n_chunks):
                rs = slice(c * GLA_CHUNK, (c + 1) * GLA_CHUNK)
                o_c = o_intra[rs] + lax.dot_general(qd[rs], st.astype(BF16), NT_DIMS,
                                                    preferred_element_type=F32)
                upd = lax.dot_general(vh[rs], kd[rs], TN_DIMS, preferred_element_type=F32)
                st = decay[c * GLA_CHUNK:c * GLA_CHUNK + 1, :] * st + upd
                gate = go_ref[rs, hs]
                silu = gate / (1.0 + jnp.exp(-gate))
                o_ref[rs, hs] = _rms(o_c, onw_ref[...]) * silu
            st_s[head] = st


def _gla(proj, glog, onw, *, batch, seq, rows):
    n = batch * seq
    nr = seq // rows
    gk = GLA_HEADS * GLA_DK
    gv = GLA_HEADS * GLA_DV
    body = functools.partial(_gla_body, rows=rows)
    return pl.pallas_call(
        body,
        grid=(batch, nr),
        in_specs=[
            pl.BlockSpec((rows, gk), lambda b, r: (b * nr + r, GQ_BLK)),
            pl.BlockSpec((rows, gk), lambda b, r: (b * nr + r, GK_BLK)),
            pl.BlockSpec((rows, gv), lambda b, r: (b * nr + r, GV_BLK)),
            pl.BlockSpec((rows, gv), lambda b, r: (b * nr + r, GO_BLK)),
            pl.BlockSpec((rows, gk), lambda b, r: (b * nr + r, 0)),
            pl.BlockSpec((1, GLA_DV), lambda b, r: (0, 0)),
        ],
        out_specs=pl.BlockSpec((rows, gv), lambda b, r: (b * nr + r, 0)),
        out_shape=jax.ShapeDtypeStruct((n, gv), F32),
        scratch_shapes=[pltpu.VMEM((GLA_HEADS, GLA_DV, LANES), F32)],
        compiler_params=_cparams(("parallel", "arbitrary")),
        name="gla",
    )(proj, proj, proj, proj, glog, onw.reshape(1, -1))


def _top_rows(s, payload=None):
    n = s.shape[0]
    rid = lax.broadcasted_iota(jnp.int32, s.shape, 0).astype(F32)
    out_row = lax.broadcasted_iota(jnp.int32, (PEER_TOPK, s.shape[1]), 0)
    vals = jnp.zeros((PEER_TOPK, s.shape[1]), F32)
    picks = jnp.zeros((PEER_TOPK, s.shape[1]), F32)
    for r in range(PEER_TOPK):
        m = jnp.max(s, axis=0, keepdims=True)
        pos = jnp.min(jnp.where(s == m, rid, float(n)), axis=0, keepdims=True)
        hit = rid == pos
        pick = pos if payload is None else jnp.sum(
            jnp.where(hit, payload, 0.0), axis=0, keepdims=True)
        vals = jnp.where(out_row == r, m, vals)
        picks = jnp.where(out_row == r, pick, picks)
        s = jnp.where(hit, -jnp.inf, s)
    return vals, picks


def _product_candidates(s_a, i_a, s_b, i_b):
    k = PEER_TOPK
    sub = lax.broadcasted_iota(jnp.int32, (8, s_a.shape[1]), 0)
    cands = [s_a[0:1] + s_b]
    ids = [i_a[0:1] * float(PEER_N_KEYS) + i_b]
    for i in range(1, 8):
        keep = k // (i + 1)
        c = s_a[i:i + 1] + s_b[0:8]
        cands.append(c if keep >= 8 else jnp.where(sub < keep, c, -jnp.inf))
        ids.append(i_a[i:i + 1] * float(PEER_N_KEYS) + i_b[0:8])
    cands.append(s_a[8:k] + s_b[0:1])
    ids.append(i_a[8:k] * float(PEER_N_KEYS) + i_b[0:1])
    return jnp.concatenate(cands, axis=0), jnp.concatenate(ids, axis=0)


def _route_body(do_ref, go_ref, x_ref, wo_ref, fnw_ref, wq_ref, keys_ref,
                x1_ref, h2_ref, idx_ref, gate_ref, q_s):
    half = DIFF_HEADS * DIFF_DV
    y = jnp.dot(do_ref[...].astype(BF16), wo_ref[:half, :], preferred_element_type=F32)
    y = y + jnp.dot(go_ref[...].astype(BF16), wo_ref[half:, :], preferred_element_type=F32)
    x1 = x_ref[...] + y
    x1_ref[...] = x1
    h2 = _rms(x1, fnw_ref[...])
    h2_ref[...] = h2
    q = jnp.dot(h2.astype(BF16), wq_ref[...], preferred_element_type=F32)
    n_groups = 2 * PEER_HEADS
    for g in range(n_groups):
        q_s[g] = q[:, g * LANES:(g + 1) * LANES].astype(BF16)

    def head_pair(hp, carry):
        for sub in range(2):
            hh = 2 * hp + sub
            tops = []
            for c in range(2):
                g = 2 * hh + c
                s = lax.dot_general(keys_ref[g], q_s[g], NT_DIMS, preferred_element_type=F32)
                tops.append(_top_rows(s))
            (s_a, i_a), (s_b, i_b) = tops
            cand, cidx = _product_candidates(s_a, i_a, s_b, i_b)
            best, expert = _top_rows(cand, cidx)
            e = jnp.exp(best - jnp.max(best, axis=0, keepdims=True))
            gate_ref[hh] = e / jnp.sum(e, axis=0, keepdims=True)
            idx_ref[hh] = expert.astype(jnp.int32)
        return carry
    lax.fori_loop(0, PEER_HEADS // 2, head_pair, 0)


def _route(diff_o, gla_o, x2d, w_out, ffn_nw, w_q, sub_keys, tm):
    n = x2d.shape[0]
    n_groups = 2 * PEER_HEADS
    keys = sub_keys.reshape(n_groups, PEER_N_KEYS, LANES).astype(BF16)
    half = DIFF_HEADS * DIFF_DV
    return pl.pallas_call(
        _route_body,
        grid=(n // tm,),
        in_specs=[
            pl.BlockSpec((tm, half), lambda i: (i, 0)),
            pl.BlockSpec((tm, half), lambda i: (i, 0)),
            pl.BlockSpec((tm, D_MODEL), lambda i: (i, 0)),
            pl.BlockSpec((D_MODEL, D_MODEL), lambda i: (0, 0)),
            pl.BlockSpec((1, D_MODEL), lambda i: (0, 0)),
            pl.BlockSpec((D_MODEL, n_groups * LANES), lambda i: (0, 0)),
            pl.BlockSpec((n_groups, PEER_N_KEYS, LANES), lambda i: (0, 0, 0)),
        ],
        out_specs=[
            pl.BlockSpec((tm, D_MODEL), lambda i: (i, 0)),
            pl.BlockSpec((tm, D_MODEL), lambda i: (i, 0)),
            pl.BlockSpec((PEER_HEADS, PEER_TOPK, tm), lambda i: (0, 0, i)),
            pl.BlockSpec((PEER_HEADS, PEER_TOPK, tm), lambda i: (0, 0, i)),
        ],
        out_shape=[
            jax.ShapeDtypeStruct((n, D_MODEL), F32),
            jax.ShapeDtypeStruct((n, D_MODEL), F32),
            jax.ShapeDtypeStruct((PEER_HEADS, PEER_TOPK, n), jnp.int32),
            jax.ShapeDtypeStruct((PEER_HEADS, PEER_TOPK, n), F32),
        ],
        scratch_shapes=[pltpu.VMEM((n_groups, tm, LANES), BF16)],
        compiler_params=_cparams(("parallel",)),
        name="peer_route",
    )(diff_o, gla_o, x2d, w_out.astype(BF16), ffn_nw.reshape(1, -1), w_q.astype(BF16), keys)


SLAB = 2 * D_MODEL // LANES
U_ROWS = D_MODEL // LANES


def _peer_body(idx_ref, h_ref, x1_ref, gate_ref, tab_ref, o_ref, buf0, buf1, sem, *, tb, grp,
               pitch):
    n_grp = tb // grp
    grp_slabs = grp * PEER_SLOTS
    bufs = (buf0, buf1)
    batch = PEER_SLOTS // (2 * U_ROWS)

    def start_copies(g, slot, tt, j0, j1):
        t = g * grp + tt
        for j in range(j0, j1):
            e = idx_ref[t, j]
            pltpu.make_async_copy(
                tab_ref.at[pl.ds(pl.multiple_of(e * SLAB, SLAB), SLAB), :],
                bufs[slot].at[pl.ds((tt * PEER_SLOTS + j) * pitch, SLAB), :],
                sem.at[slot],
            ).start(priority=j % 2)

    def wait(slot):
        pltpu.make_async_copy(tab_ref.at[pl.ds(0, grp_slabs * SLAB), :],
                              bufs[slot].at[pl.ds(0, grp_slabs * SLAB), :], sem.at[slot]).wait()

    def compute(g, slot, prefetch):
        buf = bufs[slot]
        t0 = pl.multiple_of(g * grp, grp)
        hg = h_ref[pl.ds(t0, grp), :]
        lane_g = lax.broadcasted_iota(jnp.int32, (PEER_SLOTS, grp), 1)
        act = jnp.zeros((PEER_SLOTS, grp), F32)
        for tt in range(grp):
            row0 = tt * PEER_SLOTS * pitch
            part = None
            for s in range(U_ROWS):
                if prefetch:
                    start_copies(g + 1, 1 - slot, tt, s * batch, (s + 1) * batch)
                us = buf[pl.ds(row0 + s, PEER_SLOTS, stride=pitch), :]
                term = us * hg[tt:tt + 1, s * LANES:(s + 1) * LANES]
                part = term if part is None else part + term
            act = jnp.where(lane_g == tt, jnp.sum(part, axis=1, keepdims=True), act)
        gelu = 0.5 * act * (1.0 + lax.erf(act * (2.0 ** -0.5)))
        w = gate_ref[g] * gelu
        sub_g = lax.broadcasted_iota(jnp.int32, (grp, D_MODEL), 0)
        out = jnp.zeros((grp, D_MODEL), F32)
        for tt in range(grp):
            row0 = tt * PEER_SLOTS * pitch
            wcol = w[:, tt:tt + 1]
            pieces = []
            for s in range(U_ROWS):
                if prefetch:
                    start_copies(g + 1, 1 - slot, tt, (U_ROWS + s) * batch,
                                 (U_ROWS + s + 1) * batch)
                vs = buf[pl.ds(row0 + U_ROWS + s, PEER_SLOTS, stride=pitch), :]
                pieces.append(jnp.sum(vs * wcol, axis=0, keepdims=True))
            out = jnp.where(sub_g == tt, jnp.concatenate(pieces, axis=1), out)
        o_ref[pl.ds(t0, grp), :] = x1_ref[pl.ds(t0, grp), :] + out

    def first(tt, carry):
        start_copies(0, 0, tt, 0, PEER_SLOTS)
        return carry
    lax.fori_loop(0, grp, first, 0)

    def pair(i, carry):
        wait(0)
        compute(2 * i, 0, True)
        wait(1)
        compute(2 * i + 1, 1, True)
        return carry
    assert n_grp % 2 == 0
    lax.fori_loop(0, n_grp // 2 - 1, pair, 0)
    wait(0)
    compute(n_grp - 2, 0, True)
    wait(1)
    compute(n_grp - 1, 1, False)


def _peer(idx, h2, x1, gate, table, *, tb, grp, pitch):
    n = h2.shape[0]
    body = functools.partial(_peer_body, tb=tb, grp=grp, pitch=pitch)
    return pl.pallas_call(
        body,
        grid=(n // tb,),
        in_specs=[
            pl.BlockSpec((tb, PEER_SLOTS), lambda i: (i, 0), memory_space=pltpu.SMEM),
            pl.BlockSpec((tb, D_MODEL), lambda i: (i, 0)),
            pl.BlockSpec((tb, D_MODEL), lambda i: (i, 0)),
            pl.BlockSpec((tb // grp, PEER_SLOTS, grp), lambda i: (i, 0, 0)),
            pl.BlockSpec(memory_space=pl.ANY),
        ],
        out_specs=pl.BlockSpec((tb, D_MODEL), lambda i: (i, 0)),
        out_shape=jax.ShapeDtypeStruct((n, D_MODEL), F32),
        scratch_shapes=[
            pltpu.VMEM((grp * PEER_SLOTS * pitch, LANES), F32),
            pltpu.VMEM((grp * PEER_SLOTS * pitch, LANES), F32),
            pltpu.SemaphoreType.DMA((2,)),
        ],
        compiler_params=_cparams(("arbitrary",)),
        name="peer_gather",
    )(idx, h2, x1, gate, table)


def _expert_table(u_tab, v_tab):
    e = u_tab.shape[0]
    return jnp.concatenate([u_tab, v_tab], axis=1).reshape(e * SLAB, LANES)


def _tiles(batch, seq):
    n = batch * seq
    return dict(
        tm_proj=min(256, n),
        tq=min(1024, seq),
        tk=min(512, seq),
        gla_rows=min(256, seq),
        tm_route=min(128, n),
        tb_peer=min(128, n),
        grp=8,
        pitch=20,
    )


def kernel(x, attn_norm_w, w_in, diff_q_norm_w, diff_k_norm_w, lambda_q1, lambda_k1, lambda_q2, lambda_k2, diff_out_norm_w, gla_gate_w2, gla_gate_b, gla_out_norm_w, w_out, ffn_norm_w, peer_w_q, peer_sub_keys, peer_u, peer_v):
    batch, seq, _ = x.shape
    n = batch * seq
    t = _tiles(batch, seq)
    depth = w_in.shape[0]
    x2d = x.reshape(n, D_MODEL)
    for l in range(depth):
        lam_init = 0.8 - 0.6 * math.exp(-0.3 * l)
        proj, glog = _inproj(x2d, attn_norm_w[l], w_in[l], gla_gate_w2[l], gla_gate_b[l],
                             t["tm_proj"])
        diff_o = _diff_attention(
            proj, lambda_q1[l], lambda_k1[l], lambda_q2[l], lambda_k2[l],
            diff_q_norm_w[l], diff_k_norm_w[l], diff_out_norm_w[l],
            batch=batch, seq=seq, lam_init=lam_init, tq=t["tq"], tk=t["tk"])
        gla_o = _gla(proj, glog, gla_out_norm_w[l], batch=batch, seq=seq, rows=t["gla_rows"])
        x1, h2, idx, gate = _route(diff_o, gla_o, x2d, w_out[l], ffn_norm_w[l], peer_w_q[l],
                                   peer_sub_keys[l], t["tm_route"])
        grp = t["grp"]
        idx2d = idx.reshape(PEER_SLOTS, n).T
        gate_g = gate.reshape(PEER_SLOTS, n // grp, grp).transpose(1, 0, 2)
        table = _expert_table(peer_u[l], peer_v[l])
        x2d = _peer(idx2d, h2, x1, gate_g, table, tb=t["tb_peer"], grp=grp, pitch=t["pitch"])
    return x2d.reshape(batch, seq, D_MODEL)
```

```python
import functools
import math

import jax
import jax.numpy as jnp
from jax import lax
from jax.experimental import pallas as pl
from jax.experimental.pallas import tpu as pltpu

F32 = jnp.float32
BF16 = jnp.bfloat16
HIGHEST = lax.Precision.HIGHEST

D_MODEL = 1024
DIFF_HEADS = 4
DIFF_D = 64
DIFF_DV = 128
GLA_HEADS = 4
GLA_DK = 64
GLA_DV = 128
GLA_GATE_RANK = 16
GLA_GATE_NORMALIZER = 16.0
GLA_CHUNK = 64
PEER_HEADS = 8
PEER_N_KEYS = 128
PEER_TOPK = 16
PEER_SLOTS = PEER_HEADS * PEER_TOPK
RMS_EPS = 1e-6

LANES = 128
PROJ_W = 3072
PROJ_PAD = PROJ_W + LANES

DQ_BLK, DK_BLK, DV_BLK = 0, 4, 8
GQ_BLK, GK_BLK = 6, 7
GV_BLK, GO_BLK = 4, 5

VMEM_LIMIT = 56 * 1024 * 1024

NT_DIMS = (((1,), (1,)), ((), ()))
TN_DIMS = (((0,), (0,)), ((), ()))


def _cparams(sem):
    return pltpu.CompilerParams(dimension_semantics=sem, vmem_limit_bytes=VMEM_LIMIT)


def _rms(x, w):
    ms = jnp.mean(x * x, axis=-1, keepdims=True)
    return x * lax.rsqrt(ms + RMS_EPS) * w


def _inproj_body(x_ref, nw_ref, w_ref, w2_ref, b_ref, proj_ref, glog_ref):
    h = _rms(x_ref[...], nw_ref[...])
    p = jnp.dot(h.astype(BF16), w_ref[...], preferred_element_type=F32)
    proj_ref[...] = p[:, :PROJ_W]
    g = jnp.dot(p[:, PROJ_W:], w2_ref[...], preferred_element_type=F32,
                precision=HIGHEST) + b_ref[...]
    log_sig = -(jnp.maximum(-g, 0.0) + jnp.log1p(jnp.exp(-jnp.abs(g))))
    glog_ref[...] = log_sig * (1.0 / GLA_GATE_NORMALIZER)


def _inproj(x2d, norm_w, w_in, gate_w2, gate_b, tm):
    n = x2d.shape[0]
    w_pad = jnp.pad(w_in, ((0, 0), (0, PROJ_PAD - w_in.shape[1]))).astype(BF16)
    w2_pad = jnp.pad(gate_w2, ((0, LANES - GLA_GATE_RANK), (0, 0)))
    gk = GLA_HEADS * GLA_DK
    return pl.pallas_call(
        _inproj_body,
        grid=(n // tm,),
        in_specs=[
            pl.BlockSpec((tm, D_MODEL), lambda i: (i, 0)),
            pl.BlockSpec((1, D_MODEL), lambda i: (0, 0)),
            pl.BlockSpec((D_MODEL, PROJ_PAD), lambda i: (0, 0)),
            pl.BlockSpec((LANES, gk), lambda i: (0, 0)),
            pl.BlockSpec((1, gk), lambda i: (0, 0)),
        ],
        out_specs=[
            pl.BlockSpec((tm, PROJ_W), lambda i: (i, 0)),
            pl.BlockSpec((tm, gk), lambda i: (i, 0)),
        ],
        out_shape=[
            jax.ShapeDtypeStruct((n, PROJ_W), F32),
            jax.ShapeDtypeStruct((n, gk), F32),
        ],
        compiler_params=_cparams(("parallel",)),
        name="inproj",
    )(x2d, norm_w.reshape(1, -1), w_pad, w2_pad, gate_b.reshape(1, -1))


def _half_norm(x, w_row):
    li = lax.broadcasted_iota(jnp.int32, (LANES, LANES), 0) // DIFF_D
    lj = lax.broadcasted_iota(jnp.int32, (LANES, LANES), 1) // DIFF_D
    grp = jnp.where(li == lj, 1.0 / DIFF_D, 0.0).astype(F32)
    ms = jnp.dot(x * x, grp, preferred_element_type=F32, precision=HIGHEST)
    return x * lax.rsqrt(ms + RMS_EPS) * w_row


def _attn_body(lq1_ref, lk1_ref, lq2_ref, lk2_ref, qnw_ref, knw_ref, onw_ref,
               q_ref, k_ref, v_ref, o_ref, kn_s, vt_s, m_s, l_s, acc_s,
               *, lam_init, tq, tk, prep_rows):
    qi = pl.program_id(2)
    seq = k_ref.shape[0]

    @pl.when(qi == 0)
    def _prep():
        for c in range(seq // prep_rows):
            r = slice(c * prep_rows, (c + 1) * prep_rows)
            kn_s[r, :] = _half_norm(k_ref[r, :], knw_ref[...]).astype(BF16)
            vt_s[:, r] = v_ref[r, :].T.astype(BF16)

    qn = _half_norm(q_ref[...], qnw_ref[...]) * (DIFF_D ** -0.5 * math.log2(math.e))
    lane = lax.broadcasted_iota(jnp.int32, (1, LANES), 1)
    qs = jnp.concatenate([jnp.where(lane < DIFF_D, qn, 0.0),
                          jnp.where(lane >= DIFF_D, qn, 0.0)], axis=0).astype(BF16)

    m_s[...] = jnp.full(m_s.shape, -jnp.inf, F32)
    l_s[...] = jnp.zeros(l_s.shape, F32)
    acc_s[...] = jnp.zeros(acc_s.shape, F32)

    def step(j, diag):
        r = pl.ds(pl.multiple_of(j * tk, tk), tk)
        s = lax.dot_general(kn_s[r, :], qs, NT_DIMS, preferred_element_type=F32)
        if diag is not None:
            kpos = lax.broadcasted_iota(jnp.int32, s.shape, 0) + diag * tk
            qpos = lax.broadcasted_iota(jnp.int32, s.shape, 1) & (tq - 1)
            s = jnp.where(kpos <= qpos, s, -jnp.inf)
        m_prev = m_s[...]
        m_new = jnp.maximum(m_prev, jnp.max(s, axis=0, keepdims=True))
        alpha = jnp.exp2(m_prev - m_new)
        p = jnp.exp2(s - m_new)
        l_s[...] = alpha * l_s[...] + jnp.sum(p, axis=0, keepdims=True)
        acc_s[...] = alpha * acc_s[...] + jnp.dot(vt_s[:, r], p.astype(BF16),
                                                  preferred_element_type=F32)
        m_s[...] = m_new

    per_tile = tq // tk

    def off_diag(j, carry):
        step(j, None)
        return carry
    lax.fori_loop(0, qi * per_tile, off_diag, 0)
    for d in range(per_tile):
        step(qi * per_tile + d, d)

    lam = (jnp.exp(jnp.sum(lq1_ref[...] * lk1_ref[...], axis=-1, keepdims=True))
           - jnp.exp(jnp.sum(lq2_ref[...] * lk2_ref[...], axis=-1, keepdims=True))
           + lam_init)
    o = acc_s[...] / l_s[...]
    o = o[:, :tq] - lam * o[:, tq:]
    ms = jnp.mean(o * o, axis=0, keepdims=True)
    o = o * lax.rsqrt(ms + RMS_EPS) * onw_ref[...] * (1.0 - lam_init)
    o_ref[...] = o.T


def _diff_attention(proj, lq1, lk1, lq2, lk2, qnw, knw, onw, *, batch, seq, lam_init, tq, tk):
    assert tq & (tq - 1) == 0, "the causal mask uses tq as a power of two"
    n = batch * seq
    nq = seq // tq
    prep_rows = min(512, seq)
    vec = lambda a: a.reshape(1, -1)
    tile2 = lambda a: jnp.concatenate([a, a]).reshape(1, -1)
    small = lambda w: pl.BlockSpec((1, w), lambda b, h, i: (0, 0))
    body = functools.partial(_attn_body, lam_init=lam_init, tq=tq, tk=tk, prep_rows=prep_rows)
    onw_cols = jnp.broadcast_to(onw.reshape(-1, 1), (DIFF_DV, tq))
    return pl.pallas_call(
        body,
        grid=(batch, DIFF_HEADS, nq),
        in_specs=[
            small(DIFF_D), small(DIFF_D), small(DIFF_D), small(DIFF_D),
            small(LANES), small(LANES),
            pl.BlockSpec((DIFF_DV, tq), lambda b, h, i: (0, 0)),
            pl.BlockSpec((tq, LANES), lambda b, h, i: (b * nq + i, DQ_BLK + h)),
            pl.BlockSpec((seq, LANES), lambda b, h, i: (b, DK_BLK + h)),
            pl.BlockSpec((seq, LANES), lambda b, h, i: (b, DV_BLK + h)),
        ],
        out_specs=pl.BlockSpec((tq, DIFF_DV), lambda b, h, i: (b * nq + i, h)),
        out_shape=jax.ShapeDtypeStruct((n, DIFF_HEADS * DIFF_DV), F32),
        scratch_shapes=[
            pltpu.VMEM((seq, LANES), BF16),
            pltpu.VMEM((DIFF_DV, seq), BF16),
            pltpu.VMEM((1, 2 * tq), F32),
            pltpu.VMEM((1, 2 * tq), F32),
            pltpu.VMEM((DIFF_DV, 2 * tq), F32),
        ],
        compiler_params=_cparams(("parallel", "parallel", "arbitrary")),
        name="diff_attention",
    )(vec(lq1), vec(lk1), vec(lq2), vec(lk2), tile2(qnw), tile2(knw), onw_cols,
      proj, proj, proj)


def _gla_body(q_ref, k_ref, v_ref, go_ref, g_ref, onw_ref, o_ref, st_s, *, rows):
    @pl.when(pl.program_id(1) == 0)
    def _init():
        st_s[...] = jnp.zeros(st_s.shape, F32)

    n_chunks = rows // GLA_CHUNK
    row = lax.broadcasted_iota(jnp.int32, (rows, rows), 0)
    col = lax.broadcasted_iota(jnp.int32, (rows, rows), 1)
    same_chunk = (row // GLA_CHUNK) == (col // GLA_CHUNK)
    causal = jnp.logical_and(same_chunk, col <= row)
    cum_mat = jnp.where(causal, 1.0, 0.0).astype(F32)
    tot_mat = jnp.where(same_chunk, 1.0, 0.0).astype(F32)
    lane = lax.broadcasted_iota(jnp.int32, (1, LANES), 1)

    for pair in range(GLA_HEADS // 2):
        cs = slice(pair * LANES, (pair + 1) * LANES)
        g = g_ref[:, cs]
        b = jnp.dot(cum_mat, g, preferred_element_type=F32, precision=HIGHEST)
        b_last = jnp.dot(tot_mat, g, preferred_element_type=F32, precision=HIGHEST)
        kk = k_ref[:, cs]
        q_dec = q_ref[:, cs] * (GLA_DK ** -0.5) * jnp.exp(b)
        k_inv = (kk * jnp.exp(-b)).astype(BF16)
        k_dec = kk * jnp.exp(b_last - b)
        decay = jnp.exp(b_last)
        for sub in range(2):
            head = 2 * pair + sub
            in_head = (lane >= sub * GLA_DK) & (lane < (sub + 1) * GLA_DK)
            qd = jnp.where(in_head, q_dec, 0.0).astype(BF16)
            kd = jnp.where(in_head, k_dec, 0.0).astype(BF16)
            hs = slice(head * GLA_DV, (head + 1) * GLA_DV)
            vh = v_ref[:, hs].astype(BF16)
            a = lax.dot_general(qd, k_inv, NT_DIMS, preferred_element_type=F32)
            a = jnp.where(causal, a, 0.0).astype(BF16)
            o_intra = jnp.dot(a, vh, preferred_element_type=F32)
            st = st_s[head]
            for c in range(n_chunks):
                rs = slice(c * GLA_CHUNK, (c + 1) * GLA_CHUNK)
                o_c = o_intra[rs] + lax.dot_general(qd[rs], st.astype(BF16), NT_DIMS,
                                                    preferred_element_type=F32)
                upd = lax.dot_general(vh[rs], kd[rs], TN_DIMS, preferred_element_type=F32)
                st = decay[c * GLA_CHUNK:c * GLA_CHUNK + 1, :] * st + upd
                gate = go_ref[rs, hs]
                silu = gate / (1.0 + jnp.exp(-gate))
                o_ref[rs, hs] = _rms(o_c, onw_ref[...]) * silu
            st_s[head] = st


def _gla(proj, glog, onw, *, batch, seq, rows):
    n = batch * seq
    nr = seq // rows
    gk = GLA_HEADS * GLA_DK
    gv = GLA_HEADS * GLA_DV
    body = functools.partial(_gla_body, rows=rows)
    return pl.pallas_call(
        body,
        grid=(batch, nr),
        in_specs=[
            pl.BlockSpec((rows, gk), lambda b, r: (b * nr + r, GQ_BLK)),
            pl.BlockSpec((rows, gk), lambda b, r: (b * nr + r, GK_BLK)),
            pl.BlockSpec((rows, gv), lambda b, r: (b * nr + r, GV_BLK)),
            pl.BlockSpec((rows, gv), lambda b, r: (b * nr + r, GO_BLK)),
            pl.BlockSpec((rows, gk), lambda b, r: (b * nr + r, 0)),
            pl.BlockSpec((1, GLA_DV), lambda b, r: (0, 0)),
        ],
        out_specs=pl.BlockSpec((rows, gv), lambda b, r: (b * nr + r, 0)),
        out_shape=jax.ShapeDtypeStruct((n, gv), F32),
        scratch_shapes=[pltpu.VMEM((GLA_HEADS, GLA_DV, LANES), F32)],
        compiler_params=_cparams(("parallel", "arbitrary")),
        name="gla",
    )(proj, proj, proj, proj, glog, onw.reshape(1, -1))


def _top_rows(s, payload=None):
    n = s.shape[0]
    rid = lax.broadcasted_iota(jnp.int32, s.shape, 0).astype(F32)
    out_row = lax.broadcasted_iota(jnp.int32, (PEER_TOPK, s.shape[1]), 0)
    vals = jnp.zeros((PEER_TOPK, s.shape[1]), F32)
    picks = jnp.zeros((PEER_TOPK, s.shape[1]), F32)
    for r in range(PEER_TOPK):
        m = jnp.max(s, axis=0, keepdims=True)
        pos = jnp.min(jnp.where(s == m, rid, float(n)), axis=0, keepdims=True)
        hit = rid == pos
        pick = pos if payload is None else jnp.sum(
            jnp.where(hit, payload, 0.0), axis=0, keepdims=True)
        vals = jnp.where(out_row == r, m, vals)
        picks = jnp.where(out_row == r, pick, picks)
        s = jnp.where(hit, -jnp.inf, s)
    return vals, picks


def _product_candidates(s_a, i_a, s_b, i_b):
    k = PEER_TOPK
    sub = lax.broadcasted_iota(jnp.int32, (8, s_a.shape[1]), 0)
    cands = [s_a[0:1] + s_b]
    ids = [i_a[0:1] * float(PEER_N_KEYS) + i_b]
    for i in range(1, 8):
        keep = k // (i + 1)
        c = s_a[i:i + 1] + s_b[0:8]
        cands.append(c if keep >= 8 else jnp.where(sub < keep, c, -jnp.inf))
        ids.append(i_a[i:i + 1] * float(PEER_N_KEYS) + i_b[0:8])
    cands.append(s_a[8:k] + s_b[0:1])
    ids.append(i_a[8:k] * float(PEER_N_KEYS) + i_b[0:1])
    return jnp.concatenate(cands, axis=0), jnp.concatenate(ids, axis=0)


def _route_body(do_ref, go_ref, x_ref, wo_ref, fnw_ref, wq_ref, keys_ref,
                x1_ref, h2_ref, idx_ref, gate_ref, q_s):
    half = DIFF_HEADS * DIFF_DV
    y = jnp.dot(do_ref[...].astype(BF16), wo_ref[:half, :], preferred_element_type=F32)
    y = y + jnp.dot(go_ref[...].astype(BF16), wo_ref[half:, :], preferred_element_type=F32)
    x1 = x_ref[...] + y
    x1_ref[...] = x1
    h2 = _rms(x1, fnw_ref[...])
    h2_ref[...] = h2
    q = jnp.dot(h2.astype(BF16), wq_ref[...], preferred_element_type=F32)
    n_groups = 2 * PEER_HEADS
    for g in range(n_groups):
        q_s[g] = q[:, g * LANES:(g + 1) * LANES].astype(BF16)

    def head_pair(hp, carry):
        for sub in range(2):
            hh = 2 * hp + sub
            tops = []
            for c in range(2):
                g = 2 * hh + c
                s = lax.dot_general(keys_ref[g], q_s[g], NT_DIMS, preferred_element_type=F32)
                tops.append(_top_rows(s))
            (s_a, i_a), (s_b, i_b) = tops
            cand, cidx = _product_candidates(s_a, i_a, s_b, i_b)
            best, expert = _top_rows(cand, cidx)
            e = jnp.exp(best - jnp.max(best, axis=0, keepdims=True))
            gate_ref[hh] = e / jnp.sum(e, axis=0, keepdims=True)
            idx_ref[hh] = expert.astype(jnp.int32)
        return carry
    lax.fori_loop(0, PEER_HEADS // 2, head_pair, 0)


def _route(diff_o, gla_o, x2d, w_out, ffn_nw, w_q, sub_keys, tm):
    n = x2d.shape[0]
    n_groups = 2 * PEER_HEADS
    keys = sub_keys.reshape(n_groups, PEER_N_KEYS, LANES).astype(BF16)
    half = DIFF_HEADS * DIFF_DV
    return pl.pallas_call(
        _route_body,
        grid=(n // tm,),
        in_specs=[
            pl.BlockSpec((tm, half), lambda i: (i, 0)),
            pl.BlockSpec((tm, half), lambda i: (i, 0)),
            pl.BlockSpec((tm, D_MODEL), lambda i: (i, 0)),
            pl.BlockSpec((D_MODEL, D_MODEL), lambda i: (0, 0)),
            pl.BlockSpec((1, D_MODEL), lambda i: (0, 0)),
            pl.BlockSpec((D_MODEL, n_groups * LANES), lambda i: (0, 0)),
            pl.BlockSpec((n_groups, PEER_N_KEYS, LANES), lambda i: (0, 0, 0)),
        ],
        out_specs=[
            pl.BlockSpec((tm, D_MODEL), lambda i: (i, 0)),
            pl.BlockSpec((tm, D_MODEL), lambda i: (i, 0)),
            pl.BlockSpec((PEER_HEADS, PEER_TOPK, tm), lambda i: (0, 0, i)),
            pl.BlockSpec((PEER_HEADS, PEER_TOPK, tm), lambda i: (0, 0, i)),
        ],
        out_shape=[
            jax.ShapeDtypeStruct((n, D_MODEL), F32),
            jax.ShapeDtypeStruct((n, D_MODEL), F32),
            jax.ShapeDtypeStruct((PEER_HEADS, PEER_TOPK, n), jnp.int32),
            jax.ShapeDtypeStruct((PEER_HEADS, PEER_TOPK, n), F32),
        ],
        scratch_shapes=[pltpu.VMEM((n_groups, tm, LANES), BF16)],
        compiler_params=_cparams(("parallel",)),
        name="peer_route",
    )(diff_o, gla_o, x2d, w_out.astype(BF16), ffn_nw.reshape(1, -1), w_q.astype(BF16), keys)


SLAB = D_MODEL // LANES
U_ROWS = SLAB
U_MASK = -65536


def _peer_body(idx_ref, h_ref, x1_ref, gate_ref, tab_ref, o_ref, buf0, buf1, sem, *, tb, grp,
               pitch):
    n_grp = tb // grp
    grp_slabs = grp * PEER_SLOTS
    bufs = (buf0, buf1)
    batch = PEER_SLOTS // (2 * U_ROWS)

    def start_copies(g, slot, tt, j0, j1):
        t = g * grp + tt
        for j in range(j0, j1):
            e = idx_ref[t, j]
            pltpu.make_async_copy(
                tab_ref.at[pl.ds(pl.multiple_of(e * SLAB, SLAB), SLAB), :],
                bufs[slot].at[pl.ds((tt * PEER_SLOTS + j) * pitch, SLAB), :],
                sem.at[slot],
            ).start(priority=j % 2)

    def wait(slot):
        pltpu.make_async_copy(tab_ref.at[pl.ds(0, grp_slabs * SLAB), :],
                              bufs[slot].at[pl.ds(0, grp_slabs * SLAB), :], sem.at[slot]).wait()

    def compute(g, slot, prefetch):
        buf = bufs[slot]
        t0 = pl.multiple_of(g * grp, grp)
        hg = h_ref[pl.ds(t0, grp), :]
        lane_g = lax.broadcasted_iota(jnp.int32, (PEER_SLOTS, grp), 1)
        act = jnp.zeros((PEER_SLOTS, grp), F32)
        for tt in range(grp):
            row0 = tt * PEER_SLOTS * pitch
            part = None
            for s in range(U_ROWS):
                if prefetch:
                    start_copies(g + 1, 1 - slot, tt, s * batch, (s + 1) * batch)
                ws = buf[pl.ds(row0 + s, PEER_SLOTS, stride=pitch), :]
                us = lax.bitcast_convert_type(ws & U_MASK, F32)
                term = us * hg[tt:tt + 1, s * LANES:(s + 1) * LANES]
                part = term if part is None else part + term
            act = jnp.where(lane_g == tt, jnp.sum(part, axis=1, keepdims=True), act)
        gelu = 0.5 * act * (1.0 + lax.erf(act * (2.0 ** -0.5)))
        w = gate_ref[g] * gelu
        sub_g = lax.broadcasted_iota(jnp.int32, (grp, D_MODEL), 0)
        out = jnp.zeros((grp, D_MODEL), F32)
        for tt in range(grp):
            row0 = tt * PEER_SLOTS * pitch
            wcol = w[:, tt:tt + 1]
            pieces = []
            for s in range(U_ROWS):
                if prefetch:
                    start_copies(g + 1, 1 - slot, tt, (U_ROWS + s) * batch,
                                 (U_ROWS + s + 1) * batch)
                ws = buf[pl.ds(row0 + s, PEER_SLOTS, stride=pitch), :]
                vs = lax.bitcast_convert_type(ws << 16, F32)
                pieces.append(jnp.sum(vs * wcol, axis=0, keepdims=True))
            out = jnp.where(sub_g == tt, jnp.concatenate(pieces, axis=1), out)
        o_ref[pl.ds(t0, grp), :] = x1_ref[pl.ds(t0, grp), :] + out

    def first(tt, carry):
        start_copies(0, 0, tt, 0, PEER_SLOTS)
        return carry
    lax.fori_loop(0, grp, first, 0)

    def pair(i, carry):
        wait(0)
        compute(2 * i, 0, True)
        wait(1)
        compute(2 * i + 1, 1, True)
        return carry
    assert n_grp % 2 == 0
    lax.fori_loop(0, n_grp // 2 - 1, pair, 0)
    wait(0)
    compute(n_grp - 2, 0, True)
    wait(1)
    compute(n_grp - 1, 1, False)


def _peer(idx, h2, x1, gate, table, *, tb, grp, pitch):
    n = h2.shape[0]
    body = functools.partial(_peer_body, tb=tb, grp=grp, pitch=pitch)
    return pl.pallas_call(
        body,
        grid=(n // tb,),
        in_specs=[
            pl.BlockSpec((tb, PEER_SLOTS), lambda i: (i, 0), memory_space=pltpu.SMEM),
            pl.BlockSpec((tb, D_MODEL), lambda i: (i, 0)),
            pl.BlockSpec((tb, D_MODEL), lambda i: (i, 0)),
            pl.BlockSpec((tb // grp, PEER_SLOTS, grp), lambda i: (i, 0, 0)),
            pl.BlockSpec(memory_space=pl.ANY),
        ],
        out_specs=pl.BlockSpec((tb, D_MODEL), lambda i: (i, 0)),
        out_shape=jax.ShapeDtypeStruct((n, D_MODEL), F32),
        scratch_shapes=[
            pltpu.VMEM((grp * PEER_SLOTS * pitch, LANES), jnp.int32),
            pltpu.VMEM((grp * PEER_SLOTS * pitch, LANES), jnp.int32),
            pltpu.SemaphoreType.DMA((2,)),
        ],
        compiler_params=_cparams(("arbitrary",)),
        name="peer_gather",
    )(idx, h2, x1, gate, table)


def _expert_table(u_tab, v_tab):
    e = u_tab.shape[0]
    bits = lambda a: lax.bitcast_convert_type(a.astype(BF16), jnp.uint16).astype(jnp.uint32)
    words = (bits(u_tab) << 16) | bits(v_tab)
    return lax.bitcast_convert_type(words, jnp.int32).reshape(e * SLAB, LANES)


def _tiles(batch, seq):
    n = batch * seq
    return dict(
        tm_proj=min(256, n),
        tq=min(1024, seq),
        tk=min(512, seq),
        gla_rows=min(256, seq),
        tm_route=min(128, n),
        tb_peer=min(128, n),
        grp=8,
        pitch=12,
    )


def kernel(x, attn_norm_w, w_in, diff_q_norm_w, diff_k_norm_w, lambda_q1, lambda_k1, lambda_q2, lambda_k2, diff_out_norm_w, gla_gate_w2, gla_gate_b, gla_out_norm_w, w_out, ffn_norm_w, peer_w_q, peer_sub_keys, peer_u, peer_v):
    batch, seq, _ = x.shape
    n = batch * seq
    t = _tiles(batch, seq)
    depth = w_in.shape[0]
    x2d = x.reshape(n, D_MODEL)
    for l in range(depth):
        lam_init = 0.8 - 0.6 * math.exp(-0.3 * l)
        proj, glog = _inproj(x2d, attn_norm_w[l], w_in[l], gla_gate_w2[l], gla_gate_b[l],
                             t["tm_proj"])
        diff_o = _diff_attention(
            proj, lambda_q1[l], lambda_k1[l], lambda_q2[l], lambda_k2[l],
            diff_q_norm_w[l], diff_k_norm_w[l], diff_out_norm_w[l],
            batch=batch, seq=seq, lam_init=lam_init, tq=t["tq"], tk=t["tk"])
        gla_o = _gla(proj, glog, gla_out_norm_w[l], batch=batch, seq=seq, rows=t["gla_rows"])
        x1, h2, idx, gate = _route(diff_o, gla_o, x2d, w_out[l], ffn_norm_w[l], peer_w_q[l],
                                   peer_sub_keys[l], t["tm_route"])
        grp = t["grp"]
        idx2d = idx.reshape(PEER_SLOTS, n).T
        gate_g = gate.reshape(PEER_SLOTS, n // grp, grp).transpose(1, 0, 2)
        table = _expert_table(peer_u[l], peer_v[l])
        x2d = _peer(idx2d, h2, x1, gate_g, table, tb=t["tb_peer"], grp=grp, pitch=t["pitch"])
    return x2d.reshape(batch, seq, D_MODEL)
```

```python
import functools
import math

import jax
import jax.numpy as jnp
from jax import lax
from jax.experimental import pallas as pl
from jax.experimental.pallas import tpu as pltpu

F32 = jnp.float32
BF16 = jnp.bfloat16
HIGHEST = lax.Precision.HIGHEST

D_MODEL = 1024
DIFF_HEADS = 4
DIFF_D = 64
DIFF_DV = 128
GLA_HEADS = 4
GLA_DK = 64
GLA_DV = 128
GLA_GATE_RANK = 16
GLA_GATE_NORMALIZER = 16.0
GLA_CHUNK = 64
PEER_HEADS = 8
PEER_N_KEYS = 128
PEER_TOPK = 16
PEER_SLOTS = PEER_HEADS * PEER_TOPK
RMS_EPS = 1e-6

LANES = 128
PROJ_W = 3072
PROJ_PAD = PROJ_W + LANES

DQ_BLK, DK_BLK, DV_BLK = 0, 4, 8
GQ_BLK, GK_BLK = 6, 7
GV_BLK, GO_BLK = 4, 5

VMEM_LIMIT = 56 * 1024 * 1024

NT_DIMS = (((1,), (1,)), ((), ()))
TN_DIMS = (((0,), (0,)), ((), ()))


def _cparams(sem):
    return pltpu.CompilerParams(dimension_semantics=sem, vmem_limit_bytes=VMEM_LIMIT)


def _rms(x, w):
    ms = jnp.mean(x * x, axis=-1, keepdims=True)
    return x * lax.rsqrt(ms + RMS_EPS) * w


def _inproj_body(x_ref, nw_ref, w_ref, w2_ref, b_ref, proj_ref, glog_ref):
    h = _rms(x_ref[...], nw_ref[...])
    p = jnp.dot(h.astype(BF16), w_ref[...], preferred_element_type=F32)
    proj_ref[...] = p[:, :PROJ_W]
    g = jnp.dot(p[:, PROJ_W:], w2_ref[...], preferred_element_type=F32,
                precision=HIGHEST) + b_ref[...]
    log_sig = -(jnp.maximum(-g, 0.0) + jnp.log1p(jnp.exp(-jnp.abs(g))))
    glog_ref[...] = log_sig * (1.0 / GLA_GATE_NORMALIZER)


def _inproj(x2d, norm_w, w_in, gate_w2, gate_b, tm):
    n = x2d.shape[0]
    w_pad = jnp.pad(w_in, ((0, 0), (0, PROJ_PAD - w_in.shape[1]))).astype(BF16)
    w2_pad = jnp.pad(gate_w2, ((0, LANES - GLA_GATE_RANK), (0, 0)))
    gk = GLA_HEADS * GLA_DK
    return pl.pallas_call(
        _inproj_body,
        grid=(n // tm,),
        in_specs=[
            pl.BlockSpec((tm, D_MODEL), lambda i: (i, 0)),
            pl.BlockSpec((1, D_MODEL), lambda i: (0, 0)),
            pl.BlockSpec((D_MODEL, PROJ_PAD), lambda i: (0, 0)),
            pl.BlockSpec((LANES, gk), lambda i: (0, 0)),
            pl.BlockSpec((1, gk), lambda i: (0, 0)),
        ],
        out_specs=[
            pl.BlockSpec((tm, PROJ_W), lambda i: (i, 0)),
            pl.BlockSpec((tm, gk), lambda i: (i, 0)),
        ],
        out_shape=[
            jax.ShapeDtypeStruct((n, PROJ_W), F32),
            jax.ShapeDtypeStruct((n, gk), F32),
        ],
        compiler_params=_cparams(("parallel",)),
        name="inproj",
    )(x2d, norm_w.reshape(1, -1), w_pad, w2_pad, gate_b.reshape(1, -1))


def _half_norm(x, w_row):
    li = lax.broadcasted_iota(jnp.int32, (LANES, LANES), 0) // DIFF_D
    lj = lax.broadcasted_iota(jnp.int32, (LANES, LANES), 1) // DIFF_D
    grp = jnp.where(li == lj, 1.0 / DIFF_D, 0.0).astype(F32)
    ms = jnp.dot(x * x, grp, preferred_element_type=F32, precision=HIGHEST)
    return x * lax.rsqrt(ms + RMS_EPS) * w_row


def _attn_body(lq1_ref, lk1_ref, lq2_ref, lk2_ref, qnw_ref, knw_ref, onw_ref,
               q_ref, k_ref, v_ref, o_ref, kn_s, vt_s, m_s, l_s, acc_s,
               *, lam_init, tq, tk, prep_rows):
    qi = pl.program_id(2)
    seq = k_ref.shape[0]

    @pl.when(qi == 0)
    def _prep():
        for c in range(seq // prep_rows):
            r = slice(c * prep_rows, (c + 1) * prep_rows)
            kn_s[r, :] = _half_norm(k_ref[r, :], knw_ref[...]).astype(BF16)
            vt_s[:, r] = v_ref[r, :].T.astype(BF16)

    qn = _half_norm(q_ref[...], qnw_ref[...]) * (DIFF_D ** -0.5 * math.log2(math.e))
    lane = lax.broadcasted_iota(jnp.int32, (1, LANES), 1)
    qs = jnp.concatenate([jnp.where(lane < DIFF_D, qn, 0.0),
                          jnp.where(lane >= DIFF_D, qn, 0.0)], axis=0).astype(BF16)

    m_s[...] = jnp.full(m_s.shape, -jnp.inf, F32)
    l_s[...] = jnp.zeros(l_s.shape, F32)
    acc_s[...] = jnp.zeros(acc_s.shape, F32)

    def step(j, diag):
        r = pl.ds(pl.multiple_of(j * tk, tk), tk)
        s = lax.dot_general(kn_s[r, :], qs, NT_DIMS, preferred_element_type=F32)
        if diag is not None:
            kpos = lax.broadcasted_iota(jnp.int32, s.shape, 0) + diag * tk
            qpos = lax.broadcasted_iota(jnp.int32, s.shape, 1) & (tq - 1)
            s = jnp.where(kpos <= qpos, s, -jnp.inf)
        m_prev = m_s[...]
        m_new = jnp.maximum(m_prev, jnp.max(s, axis=0, keepdims=True))
        alpha = jnp.exp2(m_prev - m_new)
        p = jnp.exp2(s - m_new)
        l_s[...] = alpha * l_s[...] + jnp.sum(p, axis=0, keepdims=True)
        acc_s[...] = alpha * acc_s[...] + jnp.dot(vt_s[:, r], p.astype(BF16),
                                                  preferred_element_type=F32)
        m_s[...] = m_new

    per_tile = tq // tk

    def off_diag(j, carry):
        step(j, None)
        return carry
    lax.fori_loop(0, qi * per_tile, off_diag, 0)
    for d in range(per_tile):
        step(qi * per_tile + d, d)

    lam = (jnp.exp(jnp.sum(lq1_ref[...] * lk1_ref[...], axis=-1, keepdims=True))
           - jnp.exp(jnp.sum(lq2_ref[...] * lk2_ref[...], axis=-1, keepdims=True))
           + lam_init)
    o = acc_s[...] / l_s[...]
    o = o[:, :tq] - lam * o[:, tq:]
    ms = jnp.mean(o * o, axis=0, keepdims=True)
    o = o * lax.rsqrt(ms + RMS_EPS) * onw_ref[...] * (1.0 - lam_init)
    o_ref[...] = o.T


def _diff_attention(proj, lq1, lk1, lq2, lk2, qnw, knw, onw, *, batch, seq, lam_init, tq, tk):
    assert tq & (tq - 1) == 0, "the causal mask uses tq as a power of two"
    n = batch * seq
    nq = seq // tq
    prep_rows = min(512, seq)
    vec = lambda a: a.reshape(1, -1)
    tile2 = lambda a: jnp.concatenate([a, a]).reshape(1, -1)
    small = lambda w: pl.BlockSpec((1, w), lambda b, h, i: (0, 0))
    body = functools.partial(_attn_body, lam_init=lam_init, tq=tq, tk=tk, prep_rows=prep_rows)
    onw_cols = jnp.broadcast_to(onw.reshape(-1, 1), (DIFF_DV, tq))
    return pl.pallas_call(
        body,
        grid=(batch, DIFF_HEADS, nq),
        in_specs=[
            small(DIFF_D), small(DIFF_D), small(DIFF_D), small(DIFF_D),
            small(LANES), small(LANES),
            pl.BlockSpec((DIFF_DV, tq), lambda b, h, i: (0, 0)),
            pl.BlockSpec((tq, LANES), lambda b, h, i: (b * nq + i, DQ_BLK + h)),
            pl.BlockSpec((seq, LANES), lambda b, h, i: (b, DK_BLK + h)),
            pl.BlockSpec((seq, LANES), lambda b, h, i: (b, DV_BLK + h)),
        ],
        out_specs=pl.BlockSpec((tq, DIFF_DV), lambda b, h, i: (b * nq + i, h)),
        out_shape=jax.ShapeDtypeStruct((n, DIFF_HEADS * DIFF_DV), F32),
        scratch_shapes=[
            pltpu.VMEM((seq, LANES), BF16),
            pltpu.VMEM((DIFF_DV, seq), BF16),
            pltpu.VMEM((1, 2 * tq), F32),
            pltpu.VMEM((1, 2 * tq), F32),
            pltpu.VMEM((DIFF_DV, 2 * tq), F32),
        ],
        compiler_params=_cparams(("parallel", "parallel", "arbitrary")),
        name="diff_attention",
    )(vec(lq1), vec(lk1), vec(lq2), vec(lk2), tile2(qnw), tile2(knw), onw_cols,
      proj, proj, proj)


def _gla_body(q_ref, k_ref, v_ref, go_ref, g_ref, onw_ref, o_ref, st_s, *, rows):
    @pl.when(pl.program_id(1) == 0)
    def _init():
        st_s[...] = jnp.zeros(st_s.shape, F32)

    n_chunks = rows // GLA_CHUNK
    row = lax.broadcasted_iota(jnp.int32, (rows, rows), 0)
    col = lax.broadcasted_iota(jnp.int32, (rows, rows), 1)
    same_chunk = (row // GLA_CHUNK) == (col // GLA_CHUNK)
    causal = jnp.logical_and(same_chunk, col <= row)
    cum_mat = jnp.where(causal, 1.0, 0.0).astype(F32)
    tot_mat = jnp.where(same_chunk, 1.0, 0.0).astype(F32)
    lane = lax.broadcasted_iota(jnp.int32, (1, LANES), 1)

    for pair in range(GLA_HEADS // 2):
        cs = slice(pair * LANES, (pair + 1) * LANES)
        g = g_ref[:, cs]
        b = jnp.dot(cum_mat, g, preferred_element_type=F32, precision=HIGHEST)
        b_last = jnp.dot(tot_mat, g, preferred_element_type=F32, precision=HIGHEST)
        kk = k_ref[:, cs]
        q_dec = q_ref[:, cs] * (GLA_DK ** -0.5) * jnp.exp(b)
        k_inv = (kk * jnp.exp(-b)).astype(BF16)
        k_dec = kk * jnp.exp(b_last - b)
        decay = jnp.exp(b_last)
        for sub in range(2):
            head = 2 * pair + sub
            in_head = (lane >= sub * GLA_DK) & (lane < (sub + 1) * GLA_DK)
            qd = jnp.where(in_head, q_dec, 0.0).astype(BF16)
            kd = jnp.where(in_head, k_dec, 0.0).astype(BF16)
            hs = slice(head * GLA_DV, (head + 1) * GLA_DV)
            vh = v_ref[:, hs].astype(BF16)
            a = lax.dot_general(qd, k_inv, NT_DIMS, preferred_element_type=F32)
            a = jnp.where(causal, a, 0.0).astype(BF16)
            o_intra = jnp.dot(a, vh, preferred_element_type=F32)
            st = st_s[head]
            for c in range(n_chunks):
                rs = slice(c * GLA_CHUNK, (c + 1) * GLA_CHUNK)
                o_c = o_intra[rs] + lax.dot_general(qd[rs], st.astype(BF16), NT_DIMS,
                                                    preferred_element_type=F32)
                upd = lax.dot_general(vh[rs], kd[rs], TN_DIMS, preferred_element_type=F32)
                st = decay[c * GLA_CHUNK:c * GLA_CHUNK + 1, :] * st + upd
                gate = go_ref[rs, hs]
                silu = gate / (1.0 + jnp.exp(-gate))
                o_ref[rs, hs] = _rms(o_c, onw_ref[...]) * silu
            st_s[head] = st


def _gla(proj, glog, onw, *, batch, seq, rows):
    n = batch * seq
    nr = seq // rows
    gk = GLA_HEADS * GLA_DK
    gv = GLA_HEADS * GLA_DV
    body = functools.partial(_gla_body, rows=rows)
    return pl.pallas_call(
        body,
        grid=(batch, nr),
        in_specs=[
            pl.BlockSpec((rows, gk), lambda b, r: (b * nr + r, GQ_BLK)),
            pl.BlockSpec((rows, gk), lambda b, r: (b * nr + r, GK_BLK)),
            pl.BlockSpec((rows, gv), lambda b, r: (b * nr + r, GV_BLK)),
            pl.BlockSpec((rows, gv), lambda b, r: (b * nr + r, GO_BLK)),
            pl.BlockSpec((rows, gk), lambda b, r: (b * nr + r, 0)),
            pl.BlockSpec((1, GLA_DV), lambda b, r: (0, 0)),
        ],
        out_specs=pl.BlockSpec((rows, gv), lambda b, r: (b * nr + r, 0)),
        out_shape=jax.ShapeDtypeStruct((n, gv), F32),
        scratch_shapes=[pltpu.VMEM((GLA_HEADS, GLA_DV, LANES), F32)],
        compiler_params=_cparams(("parallel", "arbitrary")),
        name="gla",
    )(proj, proj, proj, proj, glog, onw.reshape(1, -1))


def _top_rows(s, payload=None):
    n = s.shape[0]
    rid = lax.broadcasted_iota(jnp.int32, s.shape, 0).astype(F32)
    out_row = lax.broadcasted_iota(jnp.int32, (PEER_TOPK, s.shape[1]), 0)
    vals = jnp.zeros((PEER_TOPK, s.shape[1]), F32)
    picks = jnp.zeros((PEER_TOPK, s.shape[1]), F32)
    for r in range(PEER_TOPK):
        m = jnp.max(s, axis=0, keepdims=True)
        pos = jnp.min(jnp.where(s == m, rid, float(n)), axis=0, keepdims=True)
        hit = rid == pos
        pick = pos if payload is None else jnp.sum(
            jnp.where(hit, payload, 0.0), axis=0, keepdims=True)
        vals = jnp.where(out_row == r, m, vals)
        picks = jnp.where(out_row == r, pick, picks)
        s = jnp.where(hit, -jnp.inf, s)
    return vals, picks


def _product_candidates(s_a, i_a, s_b, i_b):
    k = PEER_TOPK
    sub = lax.broadcasted_iota(jnp.int32, (8, s_a.shape[1]), 0)
    cands = [s_a[0:1] + s_b]
    ids = [i_a[0:1] * float(PEER_N_KEYS) + i_b]
    for i in range(1, 8):
        keep = k // (i + 1)
        c = s_a[i:i + 1] + s_b[0:8]
        cands.append(c if keep >= 8 else jnp.where(sub < keep, c, -jnp.inf))
        ids.append(i_a[i:i + 1] * float(PEER_N_KEYS) + i_b[0:8])
    cands.append(s_a[8:k] + s_b[0:1])
    ids.append(i_a[8:k] * float(PEER_N_KEYS) + i_b[0:1])
    return jnp.concatenate(cands, axis=0), jnp.concatenate(ids, axis=0)


def _route_body(do_ref, go_ref, x_ref, wo_ref, fnw_ref, wq_ref, keys_ref,
                x1_ref, h2_ref, idx_ref, gate_ref, q_s):
    half = DIFF_HEADS * DIFF_DV
    y = jnp.dot(do_ref[...].astype(BF16), wo_ref[:half, :], preferred_element_type=F32)
    y = y + jnp.dot(go_ref[...].astype(BF16), wo_ref[half:, :], preferred_element_type=F32)
    x1 = x_ref[...] + y
    x1_ref[...] = x1
    h2 = _rms(x1, fnw_ref[...])
    h2_ref[...] = h2
    q = jnp.dot(h2.astype(BF16), wq_ref[...], preferred_element_type=F32)
    n_groups = 2 * PEER_HEADS
    for g in range(n_groups):
        q_s[g] = q[:, g * LANES:(g + 1) * LANES].astype(BF16)

    def head_pair(hp, carry):
        for sub in range(2):
            hh = 2 * hp + sub
            tops = []
            for c in range(2):
                g = 2 * hh + c
                s = lax.dot_general(keys_ref[g], q_s[g], NT_DIMS, preferred_element_type=F32)
                tops.append(_top_rows(s))
            (s_a, i_a), (s_b, i_b) = tops
            cand, cidx = _product_candidates(s_a, i_a, s_b, i_b)
            best, expert = _top_rows(cand, cidx)
            e = jnp.exp(best - jnp.max(best, axis=0, keepdims=True))
            gate_ref[hh] = e / jnp.sum(e, axis=0, keepdims=True)
            idx_ref[hh] = expert.astype(jnp.int32)
        return carry
    lax.fori_loop(0, PEER_HEADS // 2, head_pair, 0)


def _route(diff_o, gla_o, x2d, w_out, ffn_nw, w_q, sub_keys, tm):
    n = x2d.shape[0]
    n_groups = 2 * PEER_HEADS
    keys = sub_keys.reshape(n_groups, PEER_N_KEYS, LANES).astype(BF16)
    half = DIFF_HEADS * DIFF_DV
    return pl.pallas_call(
        _route_body,
        grid=(n // tm,),
        in_specs=[
            pl.BlockSpec((tm, half), lambda i: (i, 0)),
            pl.BlockSpec((tm, half), lambda i: (i, 0)),
            pl.BlockSpec((tm, D_MODEL), lambda i: (i, 0)),
            pl.BlockSpec((D_MODEL, D_MODEL), lambda i: (0, 0)),
            pl.BlockSpec((1, D_MODEL), lambda i: (0, 0)),
            pl.BlockSpec((D_MODEL, n_groups * LANES), lambda i: (0, 0)),
            pl.BlockSpec((n_groups, PEER_N_KEYS, LANES), lambda i: (0, 0, 0)),
        ],
        out_specs=[
            pl.BlockSpec((tm, D_MODEL), lambda i: (i, 0)),
            pl.BlockSpec((tm, D_MODEL), lambda i: (i, 0)),
            pl.BlockSpec((PEER_HEADS, PEER_TOPK, tm), lambda i: (0, 0, i)),
            pl.BlockSpec((PEER_HEADS, PEER_TOPK, tm), lambda i: (0, 0, i)),
        ],
        out_shape=[
            jax.ShapeDtypeStruct((n, D_MODEL), F32),
            jax.ShapeDtypeStruct((n, D_MODEL), F32),
            jax.ShapeDtypeStruct((PEER_HEADS, PEER_TOPK, n), jnp.int32),
            jax.ShapeDtypeStruct((PEER_HEADS, PEER_TOPK, n), F32),
        ],
        scratch_shapes=[pltpu.VMEM((n_groups, tm, LANES), BF16)],
        compiler_params=_cparams(("parallel",)),
        name="peer_route",
    )(diff_o, gla_o, x2d, w_out.astype(BF16), ffn_nw.reshape(1, -1), w_q.astype(BF16), keys)


SLAB = D_MODEL // LANES
U_ROWS = SLAB
U_MASK = -65536


def _peer_body(idx_ref, h_ref, x1_ref, gate_ref, tab_ref, o_ref, buf0, buf1, sem, *, tb, grp,
               pitch):
    n_grp = tb // grp
    grp_slabs = grp * PEER_SLOTS
    bufs = (buf0, buf1)
    batch = PEER_SLOTS // (2 * U_ROWS)

    def start_copies(g, slot, tt, j0, j1):
        t = g * grp + tt
        for j in range(j0, j1):
            e = idx_ref[t, j]
            pltpu.make_async_copy(
                tab_ref.at[pl.ds(pl.multiple_of(e * SLAB, SLAB), SLAB), :],
                bufs[slot].at[pl.ds((tt * PEER_SLOTS + j) * pitch, SLAB), :],
                sem.at[slot],
            ).start(priority=j % 2)

    def wait(slot):
        pltpu.make_async_copy(tab_ref.at[pl.ds(0, grp_slabs * SLAB), :],
                              bufs[slot].at[pl.ds(0, grp_slabs * SLAB), :], sem.at[slot]).wait()

    def compute(g, slot):
        buf = bufs[slot]
        t0 = pl.multiple_of(g * grp, grp)
        hg = h_ref[pl.ds(t0, grp), :]
        lane_g = lax.broadcasted_iota(jnp.int32, (PEER_SLOTS, grp), 1)
        act = jnp.zeros((PEER_SLOTS, grp), F32)
        for tt in range(grp):
            row0 = tt * PEER_SLOTS * pitch
            part = None
            for s in range(U_ROWS):
                start_copies(g + 1, 1 - slot, tt, s * batch, (s + 1) * batch)
                ws = buf[pl.ds(row0 + s, PEER_SLOTS, stride=pitch), :]
                us = lax.bitcast_convert_type(ws & U_MASK, F32)
                term = us * hg[tt:tt + 1, s * LANES:(s + 1) * LANES]
                part = term if part is None else part + term
            act = jnp.where(lane_g == tt, jnp.sum(part, axis=1, keepdims=True), act)
        gelu = 0.5 * act * (1.0 + lax.erf(act * (2.0 ** -0.5)))
        w = gate_ref[g] * gelu
        sub_g = lax.broadcasted_iota(jnp.int32, (grp, D_MODEL), 0)
        out = jnp.zeros((grp, D_MODEL), F32)
        for tt in range(grp):
            row0 = tt * PEER_SLOTS * pitch
            wcol = w[:, tt:tt + 1]
            pieces = []
            for s in range(U_ROWS):
                start_copies(g + 1, 1 - slot, tt, (U_ROWS + s) * batch,
                             (U_ROWS + s + 1) * batch)
                ws = buf[pl.ds(row0 + s, PEER_SLOTS, stride=pitch), :]
                vs = lax.bitcast_convert_type(ws << 16, F32)
                pieces.append(jnp.sum(vs * wcol, axis=0, keepdims=True))
            out = jnp.where(sub_g == tt, jnp.concatenate(pieces, axis=1), out)
        o_ref[pl.ds(t0, grp), :] = x1_ref[pl.ds(t0, grp), :] + out

    step = pl.program_id(0)

    @pl.when(step == 0)
    def _first_group():
        def first(tt, carry):
            start_copies(0, 0, tt, 0, PEER_SLOTS)
            return carry
        lax.fori_loop(0, grp, first, 0)

    def pair(i, carry):
        wait(0)
        compute(2 * i, 0)
        wait(1)
        compute(2 * i + 1, 1)
        return carry
    assert n_grp % 2 == 0
    lax.fori_loop(0, n_grp // 2, pair, 0)

    @pl.when(step == pl.num_programs(0) - 1)
    def _drain():
        wait(0)


def _peer(idx, h2, x1, gate, table, *, tb, grp, pitch):
    n = h2.shape[0]
    steps = n // tb
    nxt = jnp.concatenate([idx[tb:], jnp.zeros((tb, PEER_SLOTS), idx.dtype)], axis=0)
    idx_blocks = jnp.concatenate([idx.reshape(steps, tb, PEER_SLOTS),
                                  nxt.reshape(steps, tb, PEER_SLOTS)[:, :grp]], axis=1)
    body = functools.partial(_peer_body, tb=tb, grp=grp, pitch=pitch)
    return pl.pallas_call(
        body,
        grid=(steps,),
        in_specs=[
            pl.BlockSpec((None, tb + grp, PEER_SLOTS), lambda i: (i, 0, 0),
                         memory_space=pltpu.SMEM),
            pl.BlockSpec((tb, D_MODEL), lambda i: (i, 0)),
            pl.BlockSpec((tb, D_MODEL), lambda i: (i, 0)),
            pl.BlockSpec((tb // grp, PEER_SLOTS, grp), lambda i: (i, 0, 0)),
            pl.BlockSpec(memory_space=pl.ANY),
        ],
        out_specs=pl.BlockSpec((tb, D_MODEL), lambda i: (i, 0)),
        out_shape=jax.ShapeDtypeStruct((n, D_MODEL), F32),
        scratch_shapes=[
            pltpu.VMEM((grp * PEER_SLOTS * pitch, LANES), jnp.int32),
            pltpu.VMEM((grp * PEER_SLOTS * pitch, LANES), jnp.int32),
            pltpu.SemaphoreType.DMA((2,)),
        ],
        compiler_params=_cparams(("arbitrary",)),
        name="peer_gather",
    )(idx_blocks, h2, x1, gate, table)


def _expert_table(u_tab, v_tab):
    e = u_tab.shape[0]
    bits = lambda a: lax.bitcast_convert_type(a.astype(BF16), jnp.uint16).astype(jnp.uint32)
    words = (bits(u_tab) << 16) | bits(v_tab)
    return lax.bitcast_convert_type(words, jnp.int32).reshape(e * SLAB, LANES)


def _tiles(batch, seq):
    n = batch * seq
    return dict(
        tm_proj=min(256, n),
        tq=min(1024, seq),
        tk=min(512, seq),
        gla_rows=min(256, seq),
        tm_route=min(128, n),
        tb_peer=min(128, n),
        grp=8,
        pitch=12,
    )


def kernel(x, attn_norm_w, w_in, diff_q_norm_w, diff_k_norm_w, lambda_q1, lambda_k1, lambda_q2, lambda_k2, diff_out_norm_w, gla_gate_w2, gla_gate_b, gla_out_norm_w, w_out, ffn_norm_w, peer_w_q, peer_sub_keys, peer_u, peer_v):
    batch, seq, _ = x.shape
    n = batch * seq
    t = _tiles(batch, seq)
    depth = w_in.shape[0]
    x2d = x.reshape(n, D_MODEL)
    for l in range(depth):
        lam_init = 0.8 - 0.6 * math.exp(-0.3 * l)
        proj, glog = _inproj(x2d, attn_norm_w[l], w_in[l], gla_gate_w2[l], gla_gate_b[l],
                             t["tm_proj"])
        diff_o = _diff_attention(
            proj, lambda_q1[l], lambda_k1[l], lambda_q2[l], lambda_k2[l],
            diff_q_norm_w[l], diff_k_norm_w[l], diff_out_norm_w[l],
            batch=batch, seq=seq, lam_init=lam_init, tq=t["tq"], tk=t["tk"])
        gla_o = _gla(proj, glog, gla_out_norm_w[l], batch=batch, seq=seq, rows=t["gla_rows"])
        x1, h2, idx, gate = _route(diff_o, gla_o, x2d, w_out[l], ffn_norm_w[l], peer_w_q[l],
                                   peer_sub_keys[l], t["tm_route"])
        grp = t["grp"]
        idx2d = idx.reshape(PEER_SLOTS, n).T
        gate_g = gate.reshape(PEER_SLOTS, n // grp, grp).transpose(1, 0, 2)
        table = _expert_table(peer_u[l], peer_v[l])
        x2d = _peer(idx2d, h2, x1, gate_g, table, tb=t["tb_peer"], grp=grp, pitch=t["pitch"])
    return x2d.reshape(batch, seq, D_MODEL)
```
